```python
import jax
import jax.numpy as jnp
from jax import lax
import numpy as np


D_MODEL = 4096
BATCH = 2
SEQ = 4096
DEPTH = 4

GRID_W = 64
CTX_LEN = 256
N_MIXERS = 3
EXPAND = 2
D_INNER = EXPAND * D_MODEL
MLSTM_HEADS = 8
MLSTM_HEAD_DIM = D_INNER // MLSTM_HEADS
QKV_BLOCK = 4
CONV_W = 4
CHUNK = 64
POOL_WINDOWS = (2, 4, 8, 16)
N_GROUPS = 4
GROUP_W = D_INNER // N_GROUPS
F_BIAS_LO = 3.0
F_BIAS_HI = 6.0
POS_BASE = 10000.0
EPS = 1e-6

kernel_name = 'hybrid_mlstm_pool_fourier_prefix'


def rmsnorm(x, g):
    xf = x.astype(jnp.float32)
    y = xf * lax.rsqrt(jnp.mean(xf * xf, axis=-1, keepdims=True) + EPS)
    return (y * g.astype(jnp.float32)).astype(x.dtype)


def pos_embed_2d(n_tok, dim):
    rows = n_tok // GRID_W
    r = jnp.repeat(jnp.arange(rows, dtype=jnp.float32), GRID_W)
    col = jnp.tile(jnp.arange(GRID_W, dtype=jnp.float32), rows)
    quarter = dim // 4
    omega = 1.0 / (POS_BASE ** (jnp.arange(quarter, dtype=jnp.float32) / quarter))

    def axis_emb(p):
        a = p[:, None] * omega[None, :]
        return jnp.concatenate([jnp.sin(a), jnp.cos(a)], axis=-1)

    return jnp.concatenate([axis_emb(r), axis_emb(col)], axis=-1)


def centred_dwconv(u, w, b):
    t = u.shape[1]
    left = (CONV_W - 1) // 2
    up = jnp.pad(u, ((0, 0), (left, CONV_W - 1 - left), (0, 0)))
    out = up[:, 0:t] * w[0]
    for j in range(1, CONV_W):
        out = out + up[:, j:j + t] * w[j]
    return out + b


def blockdiag(u, w):
    bsz, t, e = u.shape
    ub = u.reshape(bsz, t, e // QKV_BLOCK, QKV_BLOCK)
    return jnp.einsum('btgi,gio->btgo', ub, w).reshape(bsz, t, e)


def to_heads(a):
    bsz, t, _ = a.shape
    return a.reshape(bsz, t, MLSTM_HEADS, MLSTM_HEAD_DIM).transpose(0, 2, 1, 3)


def flip_time(a, rev):
    return jnp.flip(a, axis=2) if rev else a


def zero_state(bsz):
    c0 = jnp.zeros((bsz, MLSTM_HEADS, MLSTM_HEAD_DIM, MLSTM_HEAD_DIM), jnp.float32)
    n0 = jnp.zeros((bsz, MLSTM_HEADS, MLSTM_HEAD_DIM), jnp.float32)
    m0 = jnp.zeros((bsz, MLSTM_HEADS), jnp.float32)
    return (c0, n0, m0)


def chunk_scan(q, k, v, log_i, log_f, state, with_h):
    bsz, nh, t, _ = q.shape
    dv = v.shape[-1]
    nc = t // CHUNK

    def to_chunks(a):
        a = a.astype(jnp.float32).reshape((bsz, nh, nc, CHUNK) + a.shape[3:])
        return jnp.moveaxis(a, 2, 0)

    xs = (to_chunks(q), to_chunks(k), to_chunks(v), to_chunks(log_i), to_chunks(log_f))
    causal = jnp.tril(jnp.ones((CHUNK, CHUNK), dtype=bool))

    def step(carry, inp):
        cmat, nvec, m = carry
        qc, kc, vc, li, lf = inp
        b = jnp.cumsum(lf, axis=-1)
        b_end = b[..., -1]
        g = b_end[..., None] - b + li
        m_new = jnp.maximum(b_end + m, jnp.max(g, axis=-1))
        wg = jnp.exp(g - m_new[..., None])
        decay = jnp.exp(b_end + m - m_new)
        c_new = decay[..., None, None] * cmat + jnp.einsum('bhsv,bhsd->bhvd', vc * wg[..., None], kc)
        n_new = decay[..., None] * nvec + jnp.einsum('bhs,bhsd->bhd', wg, kc)
        if not with_h:
            return (c_new, n_new, m_new), None
        logw = jnp.where(causal, b[..., :, None] - b[..., None, :] + li[..., None, :], -jnp.inf)
        inter = b + m[..., None]
        m_t = jnp.maximum(jnp.max(logw, axis=-1), inter)
        s = jnp.einsum('bhtd,bhsd->bhts', qc, kc) * jnp.exp(logw - m_t[..., None])
        w_inter = jnp.exp(inter - m_t)
        num = jnp.einsum('bhts,bhsv->bhtv', s, vc) + w_inter[..., None] * jnp.einsum('bhvd,bhtd->bhtv', cmat, qc)
        den = jnp.sum(s, axis=-1) + w_inter * jnp.einsum('bhd,bhtd->bht', nvec, qc)
        h = num / jnp.maximum(jnp.abs(den), jnp.exp(-m_t))[..., None]
        return (c_new, n_new, m_new), h

    state, hs = lax.scan(step, state, xs)
    if not with_h:
        return None, state
    h = jnp.moveaxis(hs, 0, 2).reshape(bsz, nh, t, dv)
    return h, state


def mlstm_branch(hl, hc, w_in, conv_w, conv_b, wq, wk, wv, w_ig, b_ig, w_fg, b_fg,
                 hnorm_w, skip, w_out, ctx_out):
    def project(h):
        u, z = jnp.split(h @ w_in, 2, axis=-1)
        xc = jax.nn.silu(centred_dwconv(u, conv_w, conv_b))
        q = blockdiag(xc, wq)
        k = blockdiag(xc, wk)
        v = blockdiag(u, wv)
        gin = jnp.concatenate([q, k, v], axis=-1)
        li = (jnp.einsum('bte,zen->zbnt', gin, w_ig) + b_ig[:, None, :, None]).astype(jnp.float32)
        lf = jax.nn.log_sigmoid(
            (jnp.einsum('bte,zen->zbnt', gin, w_fg) + b_fg[:, None, :, None]).astype(jnp.float32))
        heads = (to_heads(q), to_heads(k) * (MLSTM_HEAD_DIM ** -0.5), to_heads(v), li, lf)
        return heads, xc, z

    pc, xc_c, z_c = project(hc)
    pl, xc_l, z_l = project(hl)
    bsz_c = hc.shape[0]
    h_lat = None
    h_ctx = None
    for d in range(2):
        rev = d == 1
        hcd, st = chunk_scan(flip_time(pc[0], rev), flip_time(pc[1], rev), flip_time(pc[2], rev),
                             flip_time(pc[3][d], rev), flip_time(pc[4][d], rev),
                             zero_state(bsz_c), ctx_out)
        hld, _ = chunk_scan(flip_time(pl[0], rev), flip_time(pl[1], rev), flip_time(pl[2], rev),
                            flip_time(pl[3][d], rev), flip_time(pl[4][d], rev), st, True)
        hld = flip_time(hld, rev)
        h_lat = hld if h_lat is None else h_lat + hld
        if ctx_out:
            hcd = flip_time(hcd, rev)
            h_ctx = hcd if h_ctx is None else h_ctx + hcd

    def finish(h, xc, z):
        bsz, nh, t, dh = h.shape
        hf = jnp.transpose(h, (0, 2, 1, 3))
        mu = jnp.mean(hf, axis=-1, keepdims=True)
        var = jnp.mean(jnp.square(hf - mu), axis=-1, keepdims=True)
        hn = ((hf - mu) * lax.rsqrt(var + EPS)).reshape(bsz, t, nh * dh).astype(xc.dtype) * hnorm_w
        return ((hn + skip * xc) * jax.nn.silu(z)) @ w_out

    yl = finish(h_lat, xc_l, z_l)
    yc = finish(h_ctx, xc_c, z_c) if ctx_out else None
    return yl, yc


def pool_mix(u, w_grp):
    bsz, t, e = u.shape
    uf = u.astype(jnp.float32)
    csum = jnp.concatenate([jnp.zeros((bsz, 1, e), jnp.float32), lax.cumsum(uf, axis=1)], axis=1)
    pos = jnp.arange(t)
    outs = []
    for g, w in enumerate(POOL_WINDOWS):
        lo = w // 2
        hi = w - 1 - lo
        a = jnp.clip(pos - lo, 0, t)
        bnd = jnp.clip(pos + hi + 1, 0, t)
        cg = csum[..., g * GROUP_W:(g + 1) * GROUP_W]
        cnt = (bnd - a).astype(jnp.float32)[None, :, None]
        mean = (cg[:, bnd] - cg[:, a]) / cnt
        dlt = (mean - uf[..., g * GROUP_W:(g + 1) * GROUP_W]).astype(u.dtype)
        outs.append(dlt @ w_grp[g])
    return jnp.concatenate(outs, axis=-1)


def pool_branch(hl, hc, w_in, w_grp, scale, w_out, ctx_out):
    def run(h):
        u, z = jnp.split(h @ w_in, 2, axis=-1)
        return (pool_mix(u, w_grp) * scale * jax.nn.silu(z)) @ w_out
    return run(hl), (run(hc) if ctx_out else None)


def fourier_mix(u, w_grp):
    bsz, t, e = u.shape
    ug = u.astype(jnp.float32).reshape(bsz, t, N_GROUPS, GROUP_W)
    f = jnp.real(jnp.fft.fft2(ug, axes=(1, 3), norm='ortho')).astype(u.dtype)
    return jnp.einsum('btgi,gio->btgo', f, w_grp).reshape(bsz, t, e)


def fourier_branch(hl, hc, w_in, w_grp, w_out, ctx_out):
    def run(h):
        u, z = jnp.split(h @ w_in, 2, axis=-1)
        return (fourier_mix(u, w_grp) * jax.nn.silu(z)) @ w_out
    return run(hl), (run(hc) if ctx_out else None)


def setup_inputs(seed: int = 0) -> dict:
    key = jax.random.key(seed)
    ks = jax.random.split(key, 32)
    d = D_MODEL
    e = D_INNER
    nh = MLSTM_HEADS
    n_a = len(range(0, DEPTH, N_MIXERS))
    n_b = len(range(1, DEPTH, N_MIXERS))
    n_c = len(range(2, DEPTH, N_MIXERS))

    def nrm(k, shape, s):
        return jax.random.normal(k, shape, jnp.float32) * s

    f_bias = jnp.linspace(F_BIAS_LO, F_BIAS_HI, nh, dtype=jnp.float32)[None, None, :]
    return {
        'x': nrm(ks[0], (BATCH, SEQ, d), 1.0),
        'c': nrm(ks[1], (BATCH, d), 1.0),
        'ctx': nrm(ks[2], (BATCH, CTX_LEN, d), 1.0),
        'c_ctx': nrm(ks[3], (d,), 1.0),
        'ada_w': nrm(ks[4], (DEPTH, d, 3 * d), 0.5 * d ** -0.5),
        'ada_b': nrm(ks[5], (DEPTH, 3 * d), 0.02),
        'norm_g': 1.0 + nrm(ks[6], (DEPTH, d), 0.02),
        'final_g': 1.0 + nrm(ks[7], (d,), 0.02),
        'a_w_in': nrm(ks[8], (n_a, d, 2 * e), d ** -0.5),
        'a_conv_w': nrm(ks[9], (n_a, CONV_W, e), CONV_W ** -0.5),
        'a_conv_b': nrm(ks[10], (n_a, e), 0.02),
        'a_wq': nrm(ks[11], (n_a, e // QKV_BLOCK, QKV_BLOCK, QKV_BLOCK), QKV_BLOCK ** -0.5),
        'a_wk': nrm(ks[12], (n_a, e // QKV_BLOCK, QKV_BLOCK, QKV_BLOCK), QKV_BLOCK ** -0.5),
        'a_wv': nrm(ks[13], (n_a, e // QKV_BLOCK, QKV_BLOCK, QKV_BLOCK), QKV_BLOCK ** -0.5),
        'a_w_ig': nrm(ks[14], (n_a, 2, 3 * e, nh), (3 * e) ** -0.5),
        'a_b_ig': nrm(ks[15], (n_a, 2, nh), 0.1),
        'a_w_fg': nrm(ks[16], (n_a, 2, 3 * e, nh), (3 * e) ** -0.5),
        'a_b_fg': f_bias + nrm(ks[17], (n_a, 2, nh), 0.1),
        'a_hnorm_w': 1.0 + nrm(ks[18], (n_a, e), 0.02),
        'a_skip': 1.0 + nrm(ks[19], (n_a, e), 0.02),
        'a_w_out': nrm(ks[20], (n_a, e, d), e ** -0.5),
        'b_w_in': nrm(ks[21], (n_b, d, 2 * e), d ** -0.5),
        'b_w_grp': nrm(ks[22], (n_b, N_GROUPS, GROUP_W, GROUP_W), GROUP_W ** -0.5),
        'b_scale': 1.0 + nrm(ks[23], (n_b, e), 0.02),
        'b_w_out': nrm(ks[24], (n_b, e, d), e ** -0.5),
        'c_w_in': nrm(ks[25], (n_c, d, 2 * e), d ** -0.5),
        'c_w_grp': nrm(ks[26], (n_c, N_GROUPS, GROUP_W, GROUP_W), GROUP_W ** -0.5),
        'c_w_out': nrm(ks[27], (n_c, e, d), e ** -0.5),
    }


def reference(x, c, ctx, c_ctx, ada_w, ada_b, norm_g, final_g,
              a_w_in, a_conv_w, a_conv_b, a_wq, a_wk, a_wv, a_w_ig, a_b_ig, a_w_fg, a_b_fg,
              a_hnorm_w, a_skip, a_w_out,
              b_w_in, b_w_grp, b_scale, b_w_out,
              c_w_in, c_w_grp, c_w_out):
    x = x + pos_embed_2d(x.shape[1], x.shape[2]).astype(x.dtype)[None]
    cx = ctx
    s_lat = jax.nn.silu(c)
    s_ctx = jax.nn.silu(c_ctx)[None]
    for i in range(DEPTH):
        kind = i % N_MIXERS
        j = i // N_MIXERS
        ctx_out = i < DEPTH - 1
        shift, scale, gate = jnp.split((s_lat @ ada_w[i] + ada_b[i])[:, None, :], 3, axis=-1)
        cshift, cscale, cgate = jnp.split((s_ctx @ ada_w[i] + ada_b[i])[:, None, :], 3, axis=-1)
        hl = rmsnorm(x, norm_g[i]) * (1.0 + scale) + shift
        hc = rmsnorm(cx, norm_g[i]) * (1.0 + cscale) + cshift
        if kind == 0:
            yl, yc = mlstm_branch(hl, hc, a_w_in[j], a_conv_w[j], a_conv_b[j], a_wq[j], a_wk[j], a_wv[j],
                                  a_w_ig[j], a_b_ig[j], a_w_fg[j], a_b_fg[j], a_hnorm_w[j], a_skip[j],
                                  a_w_out[j], ctx_out)
        elif kind == 1:
            yl, yc = pool_branch(hl, hc, b_w_in[j], b_w_grp[j], b_scale[j], b_w_out[j], ctx_out)
        else:
            yl, yc = fourier_branch(hl, hc, c_w_in[j], c_w_grp[j], c_w_out[j], ctx_out)
        x = x + gate * yl
        if ctx_out:
            cx = cx + cgate * yc
    return rmsnorm(x, final_g)
```

```python
import functools
import math

import jax
import jax.numpy as jnp
from jax import lax
from jax.experimental import pallas as pl
from jax.experimental.pallas import tpu as pltpu

GRID_W = 64
N_MIXERS = 3
MLSTM_HEADS = 8
QKV_BLOCK = 4
CONV_W = 4
POOL_WINDOWS = (2, 4, 8, 16)
N_GROUPS = 4
POS_BASE = 10000.0
EPS = 1e-6

LANES = 128
VMEM_LIMIT_BYTES = 56 * 1024 * 1024
SCAN_CHUNK = 256

_F32 = jnp.float32
_BF16 = jnp.bfloat16


def _cparams(*sem):
    return pltpu.CompilerParams(dimension_semantics=sem, vmem_limit_bytes=VMEM_LIMIT_BYTES)


def _tile(n, target, align):
    best = None
    for t in range(align, min(n, target) + 1, align):
        if n % t == 0:
            best = t
    return n if best is None else best


def _silu(v):
    return v * (1.0 / (1.0 + jnp.exp(-v)))


def _matmul(grid, a, a_spec, b, b_spec, out_shape, out_spec, extras=(), extra_specs=(), epilogue=None,
            name="matmul"):
    nk = grid[-1]
    n_extra = len(extras)
    k_axis = len(grid) - 1
    if epilogue is None:
        epilogue = lambda acc, pids: acc
    tm, tn = out_spec.block_shape[-2], out_spec.block_shape[-1]

    def body(*refs):
        a_ref, b_ref = refs[0], refs[1]
        ex = refs[2:2 + n_extra]
        o_ref = refs[2 + n_extra]
        pids = tuple(pl.program_id(ax) for ax in range(len(grid)))
        av = a_ref[...]
        if av.dtype != _BF16:
            av = av.astype(_BF16)
        part = jnp.dot(av, b_ref[...], preferred_element_type=_F32)

        def finish(acc):
            o_ref[...] = epilogue(acc, pids, *ex).astype(o_ref.dtype)

        if nk == 1:
            finish(part)
            return
        acc_ref = refs[3 + n_extra]
        kk = pids[k_axis]

        @pl.when(kk == 0)
        def _():
            acc_ref[...] = part

        @pl.when(jnp.logical_and(kk > 0, kk < nk - 1))
        def _():
            acc_ref[...] += part

        @pl.when(kk == nk - 1)
        def _():
            finish(acc_ref[...] + part)

    scratch = [pltpu.VMEM((tm, tn), _F32)] if nk > 1 else []
    sem = ("parallel",) * k_axis + ("arbitrary",)
    return pl.pallas_call(
        body,
        grid=grid,
        in_specs=[a_spec, b_spec, *extra_specs],
        out_specs=out_spec,
        out_shape=out_shape,
        scratch_shapes=scratch,
        compiler_params=_cparams(*sem),
        name=name,
    )(a, b, *extras)


def _ada_kernel(s_ref, w_ref, b_ref, o_ref):
    s = _silu(s_ref[...]).astype(_BF16)
    acc = jnp.dot(s, w_ref[...].astype(_BF16), preferred_element_type=_F32)
    o_ref[...] = acc + b_ref[...]


def _ada(rows, ada_w, ada_b):
    depth, d, n3 = ada_w.shape
    nr = rows.shape[0]
    tn = _tile(n3, 512, LANES)
    return pl.pallas_call(
        _ada_kernel,
        grid=(depth, n3 // tn),
        in_specs=[
            pl.BlockSpec((nr, d), lambda i, n: (0, 0)),
            pl.BlockSpec((None, d, tn), lambda i, n: (i, 0, n)),
            pl.BlockSpec((None, 1, tn), lambda i, n: (i, 0, n)),
        ],
        out_specs=pl.BlockSpec((None, nr, tn), lambda i, n: (i, 0, n)),
        out_shape=jax.ShapeDtypeStruct((depth, nr, n3), _F32),
        compiler_params=_cparams("parallel", "parallel"),
        name="ada",
    )(rows, ada_w, ada_b.reshape(depth, 1, n3))


def _embed_kernel(x_ref, c_ref, re_ref, ce_ref, o_ref, *, n_rows, half):
    r = pl.program_id(1)

    @pl.when(r < n_rows)
    def _():
        o_ref[:, :half] = x_ref[:, :half] + re_ref[...]
        o_ref[:, half:] = x_ref[:, half:] + ce_ref[...]

    @pl.when(r >= n_rows)
    def _():
        o_ref[...] = c_ref[...]


def _embed(x, ctx):
    bsz, seq, d = x.shape
    n_ctx = ctx.shape[1]
    n_rows = seq // GRID_W
    n_cblk = n_ctx // GRID_W
    half = d // 2
    quarter = d // 4
    omega = 1.0 / (POS_BASE ** (jnp.arange(quarter, dtype=_F32) / quarter))

    def axis_emb(p):
        ang = p[:, None] * omega[None, :]
        return jnp.concatenate([jnp.sin(ang), jnp.cos(ang)], axis=-1)

    row_emb = axis_emb(jnp.arange(n_rows, dtype=_F32)).reshape(n_rows, 1, half)
    col_emb = axis_emb(jnp.arange(GRID_W, dtype=_F32))
    return pl.pallas_call(
        functools.partial(_embed_kernel, n_rows=n_rows, half=half),
        grid=(bsz, n_rows + n_cblk),
        in_specs=[
            pl.BlockSpec((None, GRID_W, d), lambda b, r: (b, jnp.minimum(r, n_rows - 1), 0)),
            pl.BlockSpec((None, GRID_W, d), lambda b, r: (b, jnp.maximum(r - n_rows, 0), 0)),
            pl.BlockSpec((None, 1, half), lambda b, r: (jnp.minimum(r, n_rows - 1), 0, 0)),
            pl.BlockSpec((GRID_W, half), lambda b, r: (0, 0)),
        ],
        out_specs=pl.BlockSpec((None, GRID_W, d), lambda b, r: (b, r, 0)),
        out_shape=jax.ShapeDtypeStruct((bsz, seq + n_ctx, d), x.dtype),
        compiler_params=_cparams("parallel", "parallel"),
        name="embed",
    )(x, ctx, row_emb, col_emb)


def _prenorm_kernel(x_ref, g_ref, sh_ref, sc_ref, o_ref):
    x = x_ref[...]
    y = x * lax.rsqrt(jnp.mean(x * x, axis=-1, keepdims=True) + EPS)
    o_ref[...] = (y * g_ref[...] * (1.0 + sc_ref[...]) + sh_ref[...]).astype(o_ref.dtype)


def _prenorm(xs, g, mods, seq):
    bsz, t_all, d = xs.shape
    tb = _tile(math.gcd(seq, t_all - seq), 256, 8)
    n_lat = seq // tb

    def mrow(b, j):
        return jnp.where(j < n_lat, b, bsz)

    return pl.pallas_call(
        _prenorm_kernel,
        grid=(bsz, t_all // tb),
        in_specs=[
            pl.BlockSpec((None, tb, d), lambda b, j: (b, j, 0)),
            pl.BlockSpec((1, d), lambda b, j: (0, 0)),
            pl.BlockSpec((None, 1, d), lambda b, j: (mrow(b, j), 0, 0)),
            pl.BlockSpec((None, 1, d), lambda b, j: (mrow(b, j), 0, 1)),
        ],
        out_specs=pl.BlockSpec((None, tb, d), lambda b, j: (b, j, 0)),
        out_shape=jax.ShapeDtypeStruct((bsz, t_all, d), _BF16),
        compiler_params=_cparams("parallel", "parallel"),
        name="prenorm",
    )(xs, g.reshape(1, d), mods, mods)


def _final_norm_kernel(x_ref, g_ref, o_ref):
    x = x_ref[...]
    y = x * lax.rsqrt(jnp.mean(x * x, axis=-1, keepdims=True) + EPS)
    o_ref[...] = y * g_ref[...]


def _final_norm(xs, g, seq):
    bsz, t_all, d = xs.shape
    tb = _tile(math.gcd(seq, t_all - seq), 256, 8)
    return pl.pallas_call(
        _final_norm_kernel,
        grid=(bsz, seq // tb),
        in_specs=[
            pl.BlockSpec((None, tb, d), lambda b, j: (b, j, 0)),
            pl.BlockSpec((1, d), lambda b, j: (0, 0)),
        ],
        out_specs=pl.BlockSpec((None, tb, d), lambda b, j: (b, j, 0)),
        out_shape=jax.ShapeDtypeStruct((bsz, seq, d), xs.dtype),
        compiler_params=_cparams("parallel", "parallel"),
        name="final_norm",
    )(xs, g.reshape(1, d))


def _in_proj(h, w):
    bsz, t_all, d = h.shape
    n = w.shape[1]
    m = bsz * t_all
    tm = _tile(m, 1088, 16)
    tn = _tile(n, 1024, LANES)
    out = _matmul(
        (n // tn, m // tm, 1),
        h.reshape(m, d), pl.BlockSpec((tm, d), lambda j, i, k: (i, 0)),
        w, pl.BlockSpec((d, tn), lambda j, i, k: (0, j)),
        jax.ShapeDtypeStruct((m, n), _BF16), pl.BlockSpec((tm, tn), lambda j, i, k: (i, j)),
        name="in_proj")
    return out.reshape(bsz, t_all, n)


def _out_proj(a, w, xs, mods, seq):
    bsz, t_all, e = a.shape
    d = w.shape[1]
    tm = _tile(t_all, 1088, 16)
    tn = _tile(d, 512, LANES)
    tk = _tile(e, 4096, LANES)
    mpb = t_all // tm
    g_off = 2 * (d // tn)

    def epilogue(acc, pids, res_ref, gl_ref, gc_ref):
        row = pids[2] * tm + lax.broadcasted_iota(jnp.int32, acc.shape, 0)
        gate = jnp.where(row < seq, gl_ref[...], gc_ref[...])
        return res_ref[...] + gate * acc

    return _matmul(
        (bsz, d // tn, mpb, e // tk),
        a, pl.BlockSpec((None, tm, tk), lambda b, j, i, k: (b, i, k)),
        w, pl.BlockSpec((tk, tn), lambda b, j, i, k: (k, j)),
        jax.ShapeDtypeStruct((bsz, t_all, d), xs.dtype),
        pl.BlockSpec((None, tm, tn), lambda b, j, i, k: (b, i, j)),
        extras=(xs, mods, mods),
        extra_specs=(
            pl.BlockSpec((None, tm, tn), lambda b, j, i, k: (b, i, j)),
            pl.BlockSpec((None, 1, tn), lambda b, j, i, k: (b, 0, g_off + j)),
            pl.BlockSpec((None, 1, tn), lambda b, j, i, k: (bsz, 0, g_off + j)),
        ),
        epilogue=epilogue, name="out_proj")


def _segment_pos(shape, seq, n_ctx):
    t = lax.broadcasted_iota(jnp.int32, shape, 0)
    is_lat = t < seq
    return jnp.where(is_lat, t, t - seq), jnp.where(is_lat, seq, n_ctx)


def _shift_rows(u, k, pos, seg_len):
    t_all = u.shape[0]
    rolled = pltpu.roll(u, (-k) % t_all, axis=0)
    valid = (pos >= -k) if k < 0 else (pos < seg_len - k)
    return jnp.where(valid, rolled, 0.0)


def _mlstm_pre_kernel(u_ref, cw_ref, cb_ref, wq_ref, wk_ref, wv_ref, wg_ref, gb_ref,
                      xc_ref, q_ref, k_ref, v_ref, g_ref, *, seq, n_ctx, k_scale, n_gate):
    j = pl.program_id(1)
    ub = u_ref[...]
    u = ub.astype(_F32)
    pos, seg_len = _segment_pos(u.shape, seq, n_ctx)
    left = (CONV_W - 1) // 2
    acc = cb_ref[...] + u * cw_ref[left:left + 1, :]
    for jj in range(CONV_W):
        if jj != left:
            acc = acc + _shift_rows(u, jj - left, pos, seg_len) * cw_ref[jj:jj + 1, :]
    xc = _silu(acc)
    xcb = xc.astype(_BF16)
    q = jnp.dot(xcb, wq_ref[...], preferred_element_type=_F32)
    k = jnp.dot(xcb, wk_ref[...], preferred_element_type=_F32)
    v = jnp.dot(ub, wv_ref[...], preferred_element_type=_F32)
    qb, kb, vb = q.astype(_BF16), k.astype(_BF16), v.astype(_BF16)
    xc_ref[...] = xcb
    q_ref[...] = qb
    k_ref[...] = (k * k_scale).astype(_BF16)
    v_ref[...] = vb
    te = ub.shape[1]
    part = (jnp.dot(qb, wg_ref[0:te, :], preferred_element_type=_F32)
            + jnp.dot(kb, wg_ref[te:2 * te, :], preferred_element_type=_F32)
            + jnp.dot(vb, wg_ref[2 * te:3 * te, :], preferred_element_type=_F32))

    @pl.when(j == 0)
    def _():
        g_ref[...] = part + gb_ref[...]

    @pl.when(j > 0)
    def _():
        g_ref[...] += part

    @pl.when(j == pl.num_programs(1) - 1)
    def _():
        g = g_ref[...]
        col = lax.broadcasted_iota(jnp.int32, g.shape, 1)
        log_sig = jnp.minimum(g, 0.0) - jnp.log1p(jnp.exp(-jnp.abs(g)))
        g_ref[...] = jnp.where(col < n_gate, g, log_sig)


def _blockdiag_tiles(w, te):
    nb = w.shape[0]
    per = te // QKV_BLOCK
    wt = w.reshape(nb // per, per, QKV_BLOCK, QKV_BLOCK)
    eye = jnp.eye(per, dtype=w.dtype)
    dense = jnp.einsum('tgio,gh->tgiho', wt, eye)
    return dense.reshape(nb // per, te, te).astype(_BF16)


def _mlstm_pre(uz, conv_w, conv_b, wq, wk, wv, w_ig, b_ig, w_fg, b_fg, seq):
    bsz, t_all, e2 = uz.shape
    e = e2 // 2
    nh = w_ig.shape[-1]
    te = LANES
    nt = e // te
    ng = 4 * nh
    wg = jnp.concatenate([w_ig[0], w_ig[1], w_fg[0], w_fg[1]], axis=-1)
    wg = wg.reshape(3, nt, te, ng).transpose(1, 0, 2, 3).reshape(nt, 3 * te, ng).astype(_BF16)
    gb = jnp.concatenate([b_ig[0], b_ig[1], b_fg[0], b_fg[1]], axis=-1).reshape(1, ng)
    tok = pl.BlockSpec((None, t_all, te), lambda b, j: (b, 0, j))
    bd = pl.BlockSpec((None, te, te), lambda b, j: (j, 0, 0))
    act = jax.ShapeDtypeStruct((bsz, t_all, e), _BF16)
    return pl.pallas_call(
        functools.partial(_mlstm_pre_kernel, seq=seq, n_ctx=t_all - seq,
                          k_scale=float((e // nh) ** -0.5), n_gate=2 * nh),
        grid=(bsz, nt),
        in_specs=[
            tok,
            pl.BlockSpec((CONV_W, te), lambda b, j: (0, j)),
            pl.BlockSpec((1, te), lambda b, j: (0, j)),
            bd, bd, bd,
            pl.BlockSpec((None, 3 * te, ng), lambda b, j: (j, 0, 0)),
            pl.BlockSpec((1, ng), lambda b, j: (0, 0)),
        ],
        out_specs=[tok, tok, tok, tok, pl.BlockSpec((None, t_all, ng), lambda b, j: (b, 0, 0))],
        out_shape=[act, act, act, act, jax.ShapeDtypeStruct((bsz, t_all, ng), _F32)],
        compiler_params=_cparams("parallel", "arbitrary"),
        name="mlstm_pre",
    )(uz, conv_w, conv_b.reshape(1, e), _blockdiag_tiles(wq, te), _blockdiag_tiles(wk, te),
      _blockdiag_tiles(wv, te), wg, gb)


def _scan_kernel(q_ref, k_ref, v_ref, li_ref, lf_ref, h_ref, c_sc, n_sc, m_sc):
    rev = pl.program_id(0)
    step = pl.program_id(3)

    @pl.when(step == 0)
    def _():
        c_sc[...] = jnp.zeros(c_sc.shape, _F32)
        n_sc[...] = jnp.zeros(n_sc.shape, _F32)
        m_sc[...] = jnp.zeros(m_sc.shape, _F32)

    q = q_ref[...]
    k = k_ref[...]
    v = v_ref[...]
    li_r = li_ref[...]
    lf_r = lf_ref[...]
    ln = q.shape[0]
    ti = lax.broadcasted_iota(jnp.int32, (ln, ln), 0)
    si = lax.broadcasted_iota(jnp.int32, (ln, ln), 1)
    sgn = 1 - 2 * rev
    incl = (si - ti) * sgn <= 0
    incl_t = (ti - si) * sgn <= 0
    eye = si == ti
    lf_b = jnp.broadcast_to(lf_r, (ln, ln))
    li_b = jnp.broadcast_to(li_r, (ln, ln))
    b_col = jnp.sum(jnp.where(incl, lf_b, 0.0), axis=1, keepdims=True)
    lf_col = jnp.sum(jnp.where(eye, lf_b, 0.0), axis=1, keepdims=True)
    li_col = jnp.sum(jnp.where(eye, li_b, 0.0), axis=1, keepdims=True)
    b_row = jnp.sum(jnp.where(incl_t, lf_col, 0.0), axis=0, keepdims=True)
    b_end = jnp.sum(lf_r, axis=1, keepdims=True)
    m_old = m_sc[...]
    g_row = b_end - b_row + li_r
    g_col = b_end - b_col + li_col
    m_new = jnp.maximum(b_end + m_old, jnp.max(g_row, axis=1, keepdims=True))
    wg_col = jnp.exp(g_col - m_new)
    decay = jnp.exp(b_end + m_old - m_new)

    logw = jnp.where(incl, b_col - b_row + li_r, -jnp.inf)
    inter = b_col + m_old
    m_t = jnp.maximum(jnp.max(logw, axis=1, keepdims=True), inter)
    s = lax.dot_general(q, k, (((1,), (1,)), ((), ())), preferred_element_type=_F32) * jnp.exp(logw - m_t)
    w_inter = jnp.exp(inter - m_t)
    c_old = c_sc[...]
    num = (jnp.dot(s.astype(_BF16), v, preferred_element_type=_F32)
           + w_inter * jnp.dot(q, c_old.astype(_BF16), preferred_element_type=_F32))
    den = (jnp.sum(s, axis=1, keepdims=True)
           + w_inter * jnp.sum(q.astype(_F32) * n_sc[...], axis=1, keepdims=True))
    h_ref[...] = num / jnp.maximum(jnp.abs(den), jnp.exp(-m_t))

    vw = (v.astype(_F32) * wg_col).astype(_BF16)
    c_sc[...] = decay * c_old + lax.dot_general(k, vw, (((0,), (0,)), ((), ())),
                                                preferred_element_type=_F32)
    n_sc[...] = decay * n_sc[...] + jnp.sum(k.astype(_F32) * wg_col, axis=0, keepdims=True)
    m_sc[...] = m_new


def _mlstm_scan(q, k, v, gates, seq, nh):
    bsz, t_all, e = q.shape
    dh = e // nh
    n_ctx = t_all - seq
    ln = _tile(math.gcd(seq, n_ctx), SCAN_CHUNK, 8)
    nc = t_all // ln
    ncl = seq // ln
    ncc = n_ctx // ln
    g = gates.reshape(bsz, nc, ln, 2, 2, nh).transpose(3, 4, 0, 5, 1, 2).reshape(2, 2, bsz, nh, nc, 1, ln)

    def chunk(d, s):
        fwd = jnp.where(s < ncc, ncl + s, s - ncc)
        bwd = jnp.where(s < ncc, ncl + ncc - 1 - s, ncl - 1 - (s - ncc))
        return jnp.where(d == 0, fwd, bwd)

    tok = pl.BlockSpec((None, ln, dh), lambda d, b, hh, s: (b, chunk(d, s), hh))
    gate = pl.BlockSpec((None, None, None, None, 1, ln), lambda d, b, hh, s: (d, b, hh, chunk(d, s), 0, 0))
    return pl.pallas_call(
        _scan_kernel,
        grid=(2, bsz, nh, nc),
        in_specs=[tok, tok, tok, gate, gate],
        out_specs=pl.BlockSpec((None, None, ln, dh), lambda d, b, hh, s: (d, b, chunk(d, s), hh)),
        out_shape=jax.ShapeDtypeStruct((2, bsz, t_all, e), _F32),
        scratch_shapes=[pltpu.VMEM((dh, dh), _F32), pltpu.VMEM((1, dh), _F32), pltpu.VMEM((1, 1), _F32)],
        compiler_params=_cparams("parallel", "parallel", "parallel", "arbitrary"),
        name="mlstm_scan",
    )(q, k, v, g[0], g[1])


def _mlstm_finish_kernel(hf_ref, hr_ref, xc_ref, z_ref, nw_ref, sk_ref, o_ref):
    h = hf_ref[...] + hr_ref[...]
    mu = jnp.mean(h, axis=-1, keepdims=True)
    hc = h - mu
    var = jnp.mean(hc * hc, axis=-1, keepdims=True)
    hn = hc * lax.rsqrt(var + EPS) * nw_ref[...]
    y = (hn + sk_ref[...] * xc_ref[...].astype(_F32)) * _silu(z_ref[...].astype(_F32))
    o_ref[...] = y.astype(o_ref.dtype)


def _mlstm_finish(h2, xc, uz, hnorm_w, skip, nh):
    bsz, t_all, e = xc.shape
    dh = e // nh
    tb = _tile(t_all, 544, 16)
    hspec = lambda d: pl.BlockSpec((None, None, tb, dh), lambda b, i, hh: (d, b, i, hh))
    tok = pl.BlockSpec((None, tb, dh), lambda b, i, hh: (b, i, hh))
    vec = pl.BlockSpec((1, dh), lambda b, i, hh: (0, hh))
    return pl.pallas_call(
        _mlstm_finish_kernel,
        grid=(bsz, t_all // tb, nh),
        in_specs=[hspec(0), hspec(1), tok,
                  pl.BlockSpec((None, tb, dh), lambda b, i, hh: (b, i, nh + hh)), vec, vec],
        out_specs=tok,
        out_shape=jax.ShapeDtypeStruct((bsz, t_all, e), _BF16),
        compiler_params=_cparams("parallel", "parallel", "parallel"),
        name="mlstm_finish",
    )(h2, h2, xc, uz, hnorm_w.reshape(1, e), skip.reshape(1, e))


def _mlstm_branch(h, w_in, conv_w, conv_b, wq, wk, wv, w_ig, b_ig, w_fg, b_fg, hnorm_w, skip, seq):
    nh = w_ig.shape[-1]
    uz = _in_proj(h, w_in.astype(_BF16))
    xc, q, k, v, gates = _mlstm_pre(uz, conv_w, conv_b, wq, wk, wv, w_ig, b_ig, w_fg, b_fg, seq)
    h2 = _mlstm_scan(q, k, v, gates, seq, nh)
    return _mlstm_finish(h2, xc, uz, hnorm_w, skip, nh)


def _pool_kernel(u_ref, o_ref, *, seq, n_ctx, tiles_per_group):
    grp = pl.program_id(1) // tiles_per_group
    u = u_ref[...].astype(_F32)
    pos, seg_len = _segment_pos(u.shape, seq, n_ctx)
    for gi, w in enumerate(POOL_WINDOWS):
        @pl.when(grp == gi)
        def _(w=w):
            lo = w // 2
            hi = w - 1 - lo
            trail = u
            lead = u
            span = 1
            while span < lo:
                trail = trail + _shift_rows(trail, -span, pos, seg_len)
                lead = lead + _shift_rows(lead, span, pos, seg_len)
                span *= 2
            win = _shift_rows(trail, -1, pos, seg_len) + lead
            cnt = jnp.minimum(pos + hi + 1, seg_len) - jnp.maximum(pos - lo, 0)
            o_ref[...] = (win / cnt.astype(_F32) - u).astype(o_ref.dtype)


def _group_linear(a, w_grp, uz, scale, name):
    bsz, t_all, e = a.shape
    ng, gw, _ = w_grp.shape
    tm = _tile(t_all, 1088, 16)
    tn = _tile(gw, 1024, LANES)
    npg = gw // tn

    def epilogue(acc, pids, z_ref, *sc):
        out = acc * _silu(z_ref[...].astype(_F32))
        return out * sc[0][...] if sc else out

    extras = [uz]
    especs = [pl.BlockSpec((None, tm, tn), lambda b, g, j, i, k: (b, i, (e // tn) + g * npg + j))]
    if scale is not None:
        extras.append(scale.reshape(1, e))
        especs.append(pl.BlockSpec((1, tn), lambda b, g, j, i, k: (0, g * npg + j)))
    return _matmul(
        (bsz, ng, npg, t_all // tm, 1),
        a, pl.BlockSpec((None, tm, gw), lambda b, g, j, i, k: (b, i, g)),
        w_grp.astype(_BF16), pl.BlockSpec((None, gw, tn), lambda b, g, j, i, k: (g, 0, j)),
        jax.ShapeDtypeStruct((bsz, t_all, e), _BF16),
        pl.BlockSpec((None, tm, tn), lambda b, g, j, i, k: (b, i, g * npg + j)),
        extras=tuple(extras), extra_specs=tuple(especs), epilogue=epilogue, name=name)


def _pool_branch(h, w_in, w_grp, scale, seq):
    bsz, t_all, _ = h.shape
    uz = _in_proj(h, w_in.astype(_BF16))
    e = uz.shape[-1] // 2
    te = _tile(e // N_GROUPS, 256, LANES)
    tok = pl.BlockSpec((None, t_all, te), lambda b, j: (b, 0, j))
    dlt = pl.pallas_call(
        functools.partial(_pool_kernel, seq=seq, n_ctx=t_all - seq, tiles_per_group=e // N_GROUPS // te),
        grid=(bsz, e // te),
        in_specs=[tok],
        out_specs=tok,
        out_shape=jax.ShapeDtypeStruct((bsz, t_all, e), _BF16),
        compiler_params=_cparams("parallel", "parallel"),
        name="pool",
    )(uz)
    return _group_linear(dlt, w_grp, uz, scale, "pool_group")


def _dft_tables(n, scale):
    idx = jnp.arange(n, dtype=jnp.int32)
    ang = ((idx[:, None] * idx[None, :]) % n).astype(_F32) * (2.0 * math.pi / n)
    return jnp.cos(ang) * scale, jnp.sin(ang) * scale


def _fourier_branch(h, w_in, w_grp, seq):
    bsz, t_all, _ = h.shape
    n_ctx = t_all - seq
    uz = _in_proj(h, w_in.astype(_BF16))
    e = uz.shape[-1] // 2
    gw = e // N_GROUPS
    cc, sc = _dft_tables(gw, gw ** -0.5)
    w1 = jnp.concatenate([cc, sc], axis=1).astype(_BF16)
    tm = _tile(t_all, 1088, 16)
    tn = _tile(gw, 1024, LANES)
    y = _matmul(
        (bsz, N_GROUPS, 2 * gw // tn, t_all // tm, 1),
        uz, pl.BlockSpec((None, tm, gw), lambda b, g, j, i, k: (b, i, g)),
        w1, pl.BlockSpec((gw, tn), lambda b, g, j, i, k: (0, j)),
        jax.ShapeDtypeStruct((bsz, t_all, 2 * e), _BF16),
        pl.BlockSpec((None, tm, tn), lambda b, g, j, i, k: (b, i, g * (2 * gw // tn) + j)),
        name="dft_chan")
    ctl, stl = _dft_tables(seq, seq ** -0.5)
    ctc, stc = _dft_tables(n_ctx, n_ctx ** -0.5)

    def blockdiag2(a, c):
        top = jnp.concatenate([a, jnp.zeros((a.shape[0], c.shape[1]), a.dtype)], axis=1)
        bot = jnp.concatenate([jnp.zeros((c.shape[0], a.shape[1]), a.dtype), c], axis=1)
        return jnp.concatenate([top, bot], axis=0)

    w2 = jnp.concatenate([blockdiag2(ctl, ctc), -blockdiag2(stl, stc)], axis=1).astype(_BF16)
    tn2 = _tile(gw, 512, LANES)
    npg = gw // tn2
    f = _matmul(
        (bsz, N_GROUPS, npg, t_all // tm, 2),
        w2, pl.BlockSpec((tm, t_all), lambda b, g, j, i, k: (i, k)),
        y, pl.BlockSpec((None, t_all, tn2), lambda b, g, j, i, k: (b, 0, (2 * g + k) * npg + j)),
        jax.ShapeDtypeStruct((bsz, t_all, e), _BF16),
        pl.BlockSpec((None, tm, tn2), lambda b, g, j, i, k: (b, i, g * npg + j)),
        name="dft_time")
    return _group_linear(f, w_grp, uz, None, "fourier_group")


def kernel(x, c, ctx, c_ctx, ada_w, ada_b, norm_g, final_g, a_w_in, a_conv_w, a_conv_b, a_wq, a_wk, a_wv,
           a_w_ig, a_b_ig, a_w_fg, a_b_fg, a_hnorm_w, a_skip, a_w_out, b_w_in, b_w_grp, b_scale, b_w_out,
           c_w_in, c_w_grp, c_w_out):
    bsz, seq, d = x.shape
    depth = ada_w.shape[0]
    n_rows = 8
    assert bsz + 1 <= n_rows
    rows = jnp.concatenate([c, c_ctx[None], jnp.zeros((n_rows - bsz - 1, d), c.dtype)], axis=0)
    mods = _ada(rows, ada_w, ada_b).reshape(depth, n_rows, 1, 3 * d)
    xs = _embed(x, ctx)
    for i in range(depth):
        kind = i % N_MIXERS
        j = i // N_MIXERS
        h = _prenorm(xs, norm_g[i], mods[i], seq)
        if kind == 0:
            a = _mlstm_branch(h, a_w_in[j], a_conv_w[j], a_conv_b[j], a_wq[j], a_wk[j], a_wv[j],
                              a_w_ig[j], a_b_ig[j], a_w_fg[j], a_b_fg[j], a_hnorm_w[j], a_skip[j], seq)
            w_out = a_w_out[j]
        elif kind == 1:
            a = _pool_branch(h, b_w_in[j], b_w_grp[j], b_scale[j], seq)
            w_out = b_w_out[j]
        else:
            a = _fourier_branch(h, c_w_in[j], c_w_grp[j], seq)
            w_out = c_w_out[j]
        xs = _out_proj(a, w_out.astype(_BF16), xs, mods[i], seq)
    return _final_norm(xs, final_g, seq)
```

```python
import functools
import math

import jax
import jax.numpy as jnp
import numpy as np
from jax import lax
from jax.experimental import pallas as pl
from jax.experimental.pallas import tpu as pltpu

GRID_W = 64
N_MIXERS = 3
MLSTM_HEADS = 8
QKV_BLOCK = 4
CONV_W = 4
POOL_WINDOWS = (2, 4, 8, 16)
N_GROUPS = 4
POS_BASE = 10000.0
EPS = 1e-6

LANES = 128
VMEM_LIMIT_BYTES = 56 * 1024 * 1024
SCAN_CHUNK = 256

_F32 = jnp.float32
_BF16 = jnp.bfloat16


def _cparams(*sem):
    return pltpu.CompilerParams(dimension_semantics=sem, vmem_limit_bytes=VMEM_LIMIT_BYTES)


def _tile(n, target, align):
    best = None
    for t in range(align, min(n, target) + 1, align):
        if n % t == 0:
            best = t
    return n if best is None else best


def _silu(v):
    return v * (1.0 / (1.0 + jnp.exp(-v)))


def _matmul(grid, a, a_spec, b, b_spec, out_shape, out_spec, extras=(), extra_specs=(), epilogue=None,
            name="matmul", first_visit=None):
    nk = grid[-1]
    n_extra = len(extras)
    k_axis = len(grid) - 1
    if epilogue is None:
        epilogue = lambda acc, pids: acc
    tm, tn = out_spec.block_shape[-2], out_spec.block_shape[-1]
    tk = b_spec.block_shape[-2]

    def body(*refs):
        a_ref, b_ref = refs[0], refs[1]
        ex = refs[2:2 + n_extra]
        o_ref = refs[2 + n_extra]
        pids = tuple(pl.program_id(ax) for ax in range(len(grid)))
        kk = pids[k_axis]
        if first_visit is None:
            bv = b_ref[...].astype(_BF16)
        else:
            w_bf = refs[-1]

            @pl.when(first_visit(pids))
            def _():
                w_bf[kk] = b_ref[...].astype(_BF16)

            bv = w_bf[kk]
        part = jnp.dot(a_ref[...].astype(_BF16), bv, preferred_element_type=_F32)

        def finish(acc):
            o_ref[...] = epilogue(acc, pids, *ex).astype(o_ref.dtype)

        if nk == 1:
            finish(part)
            return
        acc_ref = refs[3 + n_extra]

        @pl.when(kk == 0)
        def _():
            acc_ref[...] = part

        @pl.when(jnp.logical_and(kk > 0, kk < nk - 1))
        def _():
            acc_ref[...] += part

        @pl.when(kk == nk - 1)
        def _():
            finish(acc_ref[...] + part)

    scratch = [pltpu.VMEM((tm, tn), _F32)] if nk > 1 else []
    if first_visit is None:
        sem = ("parallel",) * k_axis + ("arbitrary",)
    else:
        scratch.append(pltpu.VMEM((nk, tk, tn), _BF16))
        sem = ("parallel",) + ("arbitrary",) * k_axis
    return pl.pallas_call(
        body,
        grid=grid,
        in_specs=[a_spec, b_spec, *extra_specs],
        out_specs=out_spec,
        out_shape=out_shape,
        scratch_shapes=scratch,
        compiler_params=_cparams(*sem),
        name=name,
    )(a, b, *extras)


def _ada_kernel(s_ref, w_ref, b_ref, o_ref):
    s = _silu(s_ref[...]).astype(_BF16)
    acc = jnp.dot(s, w_ref[...].astype(_BF16), preferred_element_type=_F32)
    o_ref[...] = acc + b_ref[...]


def _ada(rows, ada_w, ada_b):
    depth, d, n3 = ada_w.shape
    nr = rows.shape[0]
    tn = _tile(n3, 512, LANES)
    return pl.pallas_call(
        _ada_kernel,
        grid=(depth, n3 // tn),
        in_specs=[
            pl.BlockSpec((nr, d), lambda i, n: (0, 0)),
            pl.BlockSpec((None, d, tn), lambda i, n: (i, 0, n)),
            pl.BlockSpec((None, 1, tn), lambda i, n: (i, 0, n)),
        ],
        out_specs=pl.BlockSpec((None, nr, tn), lambda i, n: (i, 0, n)),
        out_shape=jax.ShapeDtypeStruct((depth, nr, n3), _F32),
        compiler_params=_cparams("parallel", "parallel"),
        name="ada",
    )(rows, ada_w, ada_b.reshape(depth, 1, n3))


def _embed_kernel(x_ref, c_ref, re_ref, ce_ref, o_ref, *, n_rows, half):
    r = pl.program_id(1)

    @pl.when(r < n_rows)
    def _():
        o_ref[:, :half] = x_ref[:, :half] + re_ref[...]
        o_ref[:, half:] = x_ref[:, half:] + ce_ref[...]

    @pl.when(r >= n_rows)
    def _():
        o_ref[...] = c_ref[...]


def _embed(x, ctx):
    bsz, seq, d = x.shape
    n_ctx = ctx.shape[1]
    n_rows = seq // GRID_W
    n_cblk = n_ctx // GRID_W
    half = d // 2
    quarter = d // 4
    omega = 1.0 / (POS_BASE ** (jnp.arange(quarter, dtype=_F32) / quarter))

    def axis_emb(p):
        ang = p[:, None] * omega[None, :]
        return jnp.concatenate([jnp.sin(ang), jnp.cos(ang)], axis=-1)

    row_emb = axis_emb(jnp.arange(n_rows, dtype=_F32)).reshape(n_rows, 1, half)
    col_emb = axis_emb(jnp.arange(GRID_W, dtype=_F32))
    return pl.pallas_call(
        functools.partial(_embed_kernel, n_rows=n_rows, half=half),
        grid=(bsz, n_rows + n_cblk),
        in_specs=[
            pl.BlockSpec((None, GRID_W, d), lambda b, r: (b, jnp.minimum(r, n_rows - 1), 0)),
            pl.BlockSpec((None, GRID_W, d), lambda b, r: (b, jnp.maximum(r - n_rows, 0), 0)),
            pl.BlockSpec((None, 1, half), lambda b, r: (jnp.minimum(r, n_rows - 1), 0, 0)),
            pl.BlockSpec((GRID_W, half), lambda b, r: (0, 0)),
        ],
        out_specs=pl.BlockSpec((None, GRID_W, d), lambda b, r: (b, r, 0)),
        out_shape=jax.ShapeDtypeStruct((bsz, seq + n_ctx, d), x.dtype),
        compiler_params=_cparams("parallel", "parallel"),
        name="embed",
    )(x, ctx, row_emb, col_emb)


def _prenorm_kernel(x_ref, g_ref, sh_ref, sc_ref, o_ref):
    x = x_ref[...]
    y = x * lax.rsqrt(jnp.mean(x * x, axis=-1, keepdims=True) + EPS)
    o_ref[...] = (y * g_ref[...] * (1.0 + sc_ref[...]) + sh_ref[...]).astype(o_ref.dtype)


def _prenorm(xs, g, mods, seq):
    bsz, t_all, d = xs.shape
    tb = _tile(math.gcd(seq, t_all - seq), 256, 8)
    n_lat = seq // tb

    def mrow(b, j):
        return jnp.where(j < n_lat, b, bsz)

    return pl.pallas_call(
        _prenorm_kernel,
        grid=(bsz, t_all // tb),
        in_specs=[
            pl.BlockSpec((None, tb, d), lambda b, j: (b, j, 0)),
            pl.BlockSpec((1, d), lambda b, j: (0, 0)),
            pl.BlockSpec((None, 1, d), lambda b, j: (mrow(b, j), 0, 0)),
            pl.BlockSpec((None, 1, d), lambda b, j: (mrow(b, j), 0, 1)),
        ],
        out_specs=pl.BlockSpec((None, tb, d), lambda b, j: (b, j, 0)),
        out_shape=jax.ShapeDtypeStruct((bsz, t_all, d), _BF16),
        compiler_params=_cparams("parallel", "parallel"),
        name="prenorm",
    )(xs, g.reshape(1, d), mods, mods)


def _final_norm_kernel(x_ref, g_ref, o_ref):
    x = x_ref[...]
    y = x * lax.rsqrt(jnp.mean(x * x, axis=-1, keepdims=True) + EPS)
    o_ref[...] = y * g_ref[...]


def _final_norm(xs, g, seq):
    bsz, t_all, d = xs.shape
    tb = _tile(math.gcd(seq, t_all - seq), 256, 8)
    return pl.pallas_call(
        _final_norm_kernel,
        grid=(bsz, seq // tb),
        in_specs=[
            pl.BlockSpec((None, tb, d), lambda b, j: (b, j, 0)),
            pl.BlockSpec((1, d), lambda b, j: (0, 0)),
        ],
        out_specs=pl.BlockSpec((None, tb, d), lambda b, j: (b, j, 0)),
        out_shape=jax.ShapeDtypeStruct((bsz, seq, d), xs.dtype),
        compiler_params=_cparams("parallel", "parallel"),
        name="final_norm",
    )(xs, g.reshape(1, d))


def _weight_block(tk, tn, nk):
    def imap(j, b, i, k):
        first = jnp.logical_and(b == 0, i == 0)
        return (jnp.where(first, k, nk - 1), j)
    return pl.BlockSpec((tk, tn), imap)


def _first_visit(pids):
    return jnp.logical_and(pids[1] == 0, pids[2] == 0)


def _in_proj(h, w):
    bsz, t_all, d = h.shape
    n = w.shape[1]
    tm = _tile(t_all, 1088, 16)
    tn = _tile(n, 1024, LANES)
    tk = _tile(d, 2048, LANES)
    nk = d // tk
    return _matmul(
        (n // tn, bsz, t_all // tm, nk),
        h, pl.BlockSpec((None, tm, tk), lambda j, b, i, k: (b, i, k)),
        w, _weight_block(tk, tn, nk),
        jax.ShapeDtypeStruct((bsz, t_all, n), _BF16),
        pl.BlockSpec((None, tm, tn), lambda j, b, i, k: (b, i, j)),
        name="in_proj", first_visit=_first_visit)


def _out_proj(a, w, xs, mods, seq):
    bsz, t_all, e = a.shape
    d = w.shape[1]
    tm = _tile(t_all, 1088, 16)
    tn = _tile(d, 512, LANES)
    tk = _tile(e, 2048, LANES)
    nk = e // tk
    g_off = 2 * (d // tn)

    def epilogue(acc, pids, res_ref, gl_ref, gc_ref):
        row = pids[2] * tm + lax.broadcasted_iota(jnp.int32, acc.shape, 0)
        gate = jnp.where(row < seq, gl_ref[...], gc_ref[...])
        return res_ref[...] + gate * acc

    return _matmul(
        (d // tn, bsz, t_all // tm, nk),
        a, pl.BlockSpec((None, tm, tk), lambda j, b, i, k: (b, i, k)),
        w, _weight_block(tk, tn, nk),
        jax.ShapeDtypeStruct((bsz, t_all, d), xs.dtype),
        pl.BlockSpec((None, tm, tn), lambda j, b, i, k: (b, i, j)),
        extras=(xs, mods, mods),
        extra_specs=(
            pl.BlockSpec((None, tm, tn), lambda j, b, i, k: (b, i, j)),
            pl.BlockSpec((None, 1, tn), lambda j, b, i, k: (b, 0, g_off + j)),
            pl.BlockSpec((None, 1, tn), lambda j, b, i, k: (bsz, 0, g_off + j)),
        ),
        epilogue=epilogue, name="out_proj", first_visit=_first_visit)


def _segment_pos(shape, seq, n_ctx):
    t = lax.broadcasted_iota(jnp.int32, shape, 0)
    is_lat = t < seq
    return jnp.where(is_lat, t, t - seq), jnp.where(is_lat, seq, n_ctx)


def _shift_rows(u, k, pos, seg_len):
    t_all = u.shape[0]
    rolled = pltpu.roll(u, (-k) % t_all, axis=0)
    valid = (pos >= -k) if k < 0 else (pos < seg_len - k)
    return jnp.where(valid, rolled, 0.0)


def _mlstm_pre_kernel(u_ref, cw_ref, cb_ref, wq_ref, wk_ref, wv_ref, wg_ref, gb_ref,
                      xc_ref, q_ref, k_ref, v_ref, g_ref, *, seq, n_ctx, k_scale, n_gate):
    j = pl.program_id(1)
    ub = u_ref[...]
    u = ub.astype(_F32)
    pos, seg_len = _segment_pos(u.shape, seq, n_ctx)
    left = (CONV_W - 1) // 2
    acc = cb_ref[...] + u * cw_ref[left:left + 1, :]
    for jj in range(CONV_W):
        if jj != left:
            acc = acc + _shift_rows(u, jj - left, pos, seg_len) * cw_ref[jj:jj + 1, :]
    xc = _silu(acc)
    xcb = xc.astype(_BF16)
    q = jnp.dot(xcb, wq_ref[...], preferred_element_type=_F32)
    k = jnp.dot(xcb, wk_ref[...], preferred_element_type=_F32)
    v = jnp.dot(ub, wv_ref[...], preferred_element_type=_F32)
    qb, kb, vb = q.astype(_BF16), k.astype(_BF16), v.astype(_BF16)
    xc_ref[...] = xcb
    q_ref[...] = qb
    k_ref[...] = (k * k_scale).astype(_BF16)
    v_ref[...] = vb
    part = jnp.dot(jnp.concatenate([qb, kb, vb], axis=1), wg_ref[...], preferred_element_type=_F32)

    @pl.when(j == 0)
    def _():
        g_ref[...] = part + gb_ref[...]

    @pl.when(j > 0)
    def _():
        g_ref[...] += part

    @pl.when(j == pl.num_programs(1) - 1)
    def _():
        g = g_ref[...]
        col = lax.broadcasted_iota(jnp.int32, g.shape, 1)
        log_sig = jnp.minimum(g, 0.0) - jnp.log1p(jnp.exp(-jnp.abs(g)))
        g_ref[...] = jnp.where(col < n_gate, g, log_sig)


def _blockdiag_tiles(w, te):
    nb = w.shape[0]
    per = te // QKV_BLOCK
    wt = w.reshape(nb // per, per, QKV_BLOCK, QKV_BLOCK)
    eye = jnp.eye(per, dtype=w.dtype)
    dense = jnp.einsum('tgio,gh->tgiho', wt, eye)
    return dense.reshape(nb // per, te, te).astype(_BF16)


def _mlstm_pre(uz, conv_w, conv_b, wq, wk, wv, w_ig, b_ig, w_fg, b_fg, seq):
    bsz, t_all, e2 = uz.shape
    e = e2 // 2
    nh = w_ig.shape[-1]
    te = LANES
    nt = e // te
    ng = 4 * nh
    wg = jnp.concatenate([w_ig[0], w_ig[1], w_fg[0], w_fg[1]], axis=-1)
    wg = wg.reshape(3, nt, te, ng).transpose(1, 0, 2, 3).reshape(nt, 3 * te, ng).astype(_BF16)
    gb = jnp.concatenate([b_ig[0], b_ig[1], b_fg[0], b_fg[1]], axis=-1).reshape(1, ng)
    tok = pl.BlockSpec((None, t_all, te), lambda b, j: (b, 0, j))
    bd = pl.BlockSpec((None, te, te), lambda b, j: (j, 0, 0))
    act = jax.ShapeDtypeStruct((bsz, t_all, e), _BF16)
    return pl.pallas_call(
        functools.partial(_mlstm_pre_kernel, seq=seq, n_ctx=t_all - seq,
                          k_scale=float((e // nh) ** -0.5), n_gate=2 * nh),
        grid=(bsz, nt),
        in_specs=[
            tok,
            pl.BlockSpec((CONV_W, te), lambda b, j: (0, j)),
            pl.BlockSpec((1, te), lambda b, j: (0, j)),
            bd, bd, bd,
            pl.BlockSpec((None, 3 * te, ng), lambda b, j: (j, 0, 0)),
            pl.BlockSpec((1, ng), lambda b, j: (0, 0)),
        ],
        out_specs=[tok, tok, tok, tok, pl.BlockSpec((None, t_all, ng), lambda b, j: (b, 0, 0))],
        out_shape=[act, act, act, act, jax.ShapeDtypeStruct((bsz, t_all, ng), _F32)],
        compiler_params=_cparams("parallel", "arbitrary"),
        name="mlstm_pre",
    )(uz, conv_w, conv_b.reshape(1, e), _blockdiag_tiles(wq, te), _blockdiag_tiles(wk, te),
      _blockdiag_tiles(wv, te), wg, gb)


def _scan_kernel(q_ref, k_ref, v_ref, li_ref, lf_ref, h_ref, c_sc, n_sc, m_sc):
    rev = pl.program_id(0)
    step = pl.program_id(3)

    @pl.when(step == 0)
    def _():
        c_sc[...] = jnp.zeros(c_sc.shape, _F32)
        n_sc[...] = jnp.zeros(n_sc.shape, _F32)
        m_sc[...] = jnp.zeros(m_sc.shape, _F32)

    q = q_ref[...]
    k = k_ref[...]
    v = v_ref[...]
    li_r = li_ref[...]
    lf_r = lf_ref[...]
    ln = q.shape[0]
    ti = lax.broadcasted_iota(jnp.int32, (ln, ln), 0)
    si = lax.broadcasted_iota(jnp.int32, (ln, ln), 1)
    sgn = 1 - 2 * rev
    incl = (si - ti) * sgn <= 0
    incl_t = (ti - si) * sgn <= 0
    eye = si == ti
    lf_b = jnp.broadcast_to(lf_r, (ln, ln))
    li_b = jnp.broadcast_to(li_r, (ln, ln))
    b_col = jnp.sum(jnp.where(incl, lf_b, 0.0), axis=1, keepdims=True)
    lf_col = jnp.sum(jnp.where(eye, lf_b, 0.0), axis=1, keepdims=True)
    li_col = jnp.sum(jnp.where(eye, li_b, 0.0), axis=1, keepdims=True)
    b_row = jnp.sum(jnp.where(incl_t, lf_col, 0.0), axis=0, keepdims=True)
    b_end = jnp.sum(lf_r, axis=1, keepdims=True)
    m_old = m_sc[...]
    g_row = b_end - b_row + li_r
    g_col = b_end - b_col + li_col
    m_new = jnp.maximum(b_end + m_old, jnp.max(g_row, axis=1, keepdims=True))
    wg_col = jnp.exp(g_col - m_new)
    decay = jnp.exp(b_end + m_old - m_new)

    logw = jnp.where(incl, b_col - b_row + li_r, -jnp.inf)
    inter = b_col + m_old
    m_t = jnp.maximum(jnp.max(logw, axis=1, keepdims=True), inter)
    s = lax.dot_general(q, k, (((1,), (1,)), ((), ())), preferred_element_type=_F32) * jnp.exp(logw - m_t)
    w_inter = jnp.exp(inter - m_t)
    c_old = c_sc[...]
    num = (jnp.dot(s.astype(_BF16), v, preferred_element_type=_F32)
           + w_inter * jnp.dot(q, c_old.astype(_BF16), preferred_element_type=_F32))
    den = (jnp.sum(s, axis=1, keepdims=True)
           + w_inter * jnp.sum(q.astype(_F32) * n_sc[...], axis=1, keepdims=True))
    h_ref[...] = (num / jnp.maximum(jnp.abs(den), jnp.exp(-m_t))).astype(h_ref.dtype)

    vw = (v.astype(_F32) * wg_col).astype(_BF16)
    c_sc[...] = decay * c_old + lax.dot_general(k, vw, (((0,), (0,)), ((), ())),
                                                preferred_element_type=_F32)
    n_sc[...] = decay * n_sc[...] + jnp.sum(k.astype(_F32) * wg_col, axis=0, keepdims=True)
    m_sc[...] = m_new


def _mlstm_scan(q, k, v, gates, seq, nh):
    bsz, t_all, e = q.shape
    dh = e // nh
    n_ctx = t_all - seq
    ln = _tile(math.gcd(seq, n_ctx), SCAN_CHUNK, 8)
    nc = t_all // ln
    ncl = seq // ln
    ncc = n_ctx // ln
    g = gates.reshape(bsz, nc, ln, 2, 2, nh).transpose(3, 4, 0, 5, 1, 2).reshape(2, 2, bsz, nh, nc, 1, ln)

    def chunk(d, s):
        fwd = jnp.where(s < ncc, ncl + s, s - ncc)
        bwd = jnp.where(s < ncc, ncl + ncc - 1 - s, ncl - 1 - (s - ncc))
        return jnp.where(d == 0, fwd, bwd)

    tok = pl.BlockSpec((None, ln, dh), lambda d, b, hh, s: (b, chunk(d, s), hh))
    gate = pl.BlockSpec((None, None, None, None, 1, ln), lambda d, b, hh, s: (d, b, hh, chunk(d, s), 0, 0))
    return pl.pallas_call(
        _scan_kernel,
        grid=(2, bsz, nh, nc),
        in_specs=[tok, tok, tok, gate, gate],
        out_specs=pl.BlockSpec((None, None, ln, dh), lambda d, b, hh, s: (d, b, chunk(d, s), hh)),
        out_shape=jax.ShapeDtypeStruct((2, bsz, t_all, e), _BF16),
        scratch_shapes=[pltpu.VMEM((dh, dh), _F32), pltpu.VMEM((1, dh), _F32), pltpu.VMEM((1, 1), _F32)],
        compiler_params=_cparams("parallel", "parallel", "parallel", "arbitrary"),
        name="mlstm_scan",
    )(q, k, v, g[0], g[1])


def _mlstm_finish_kernel(hf_ref, hr_ref, xc_ref, z_ref, nw_ref, sk_ref, o_ref):
    h = hf_ref[...].astype(_F32) + hr_ref[...].astype(_F32)
    mu = jnp.mean(h, axis=-1, keepdims=True)
    hc = h - mu
    var = jnp.mean(hc * hc, axis=-1, keepdims=True)
    hn = hc * lax.rsqrt(var + EPS) * nw_ref[...]
    y = (hn + sk_ref[...] * xc_ref[...].astype(_F32)) * _silu(z_ref[...].astype(_F32))
    o_ref[...] = y.astype(o_ref.dtype)


def _mlstm_finish(h2, xc, uz, hnorm_w, skip, nh):
    bsz, t_all, e = xc.shape
    dh = e // nh
    tb = _tile(t_all, 544, 16)
    hspec = lambda d: pl.BlockSpec((None, None, tb, dh), lambda b, i, hh: (d, b, i, hh))
    tok = pl.BlockSpec((None, tb, dh), lambda b, i, hh: (b, i, hh))
    vec = pl.BlockSpec((1, dh), lambda b, i, hh: (0, hh))
    return pl.pallas_call(
        _mlstm_finish_kernel,
        grid=(bsz, t_all // tb, nh),
        in_specs=[hspec(0), hspec(1), tok,
                  pl.BlockSpec((None, tb, dh), lambda b, i, hh: (b, i, nh + hh)), vec, vec],
        out_specs=tok,
        out_shape=jax.ShapeDtypeStruct((bsz, t_all, e), _BF16),
        compiler_params=_cparams("parallel", "parallel", "parallel"),
        name="mlstm_finish",
    )(h2, h2, xc, uz, hnorm_w.reshape(1, e), skip.reshape(1, e))


def _mlstm_branch(h, w_in, conv_w, conv_b, wq, wk, wv, w_ig, b_ig, w_fg, b_fg, hnorm_w, skip, seq):
    nh = w_ig.shape[-1]
    uz = _in_proj(h, w_in)
    xc, q, k, v, gates = _mlstm_pre(uz, conv_w, conv_b, wq, wk, wv, w_ig, b_ig, w_fg, b_fg, seq)
    h2 = _mlstm_scan(q, k, v, gates, seq, nh)
    return _mlstm_finish(h2, xc, uz, hnorm_w, skip, nh)


def _pool_kernel(u_ref, o_ref, *, seq, n_ctx, tiles_per_group):
    grp = pl.program_id(1) // tiles_per_group
    u = u_ref[...].astype(_F32)
    pos, seg_len = _segment_pos(u.shape, seq, n_ctx)
    for gi, w in enumerate(POOL_WINDOWS):
        @pl.when(grp == gi)
        def _(w=w):
            lo = w // 2
            hi = w - 1 - lo
            trail = u
            lead = u
            span = 1
            while span < lo:
                trail = trail + _shift_rows(trail, -span, pos, seg_len)
                lead = lead + _shift_rows(lead, span, pos, seg_len)
                span *= 2
            win = _shift_rows(trail, -1, pos, seg_len) + lead
            cnt = jnp.minimum(pos + hi + 1, seg_len) - jnp.maximum(pos - lo, 0)
            o_ref[...] = (win / cnt.astype(_F32) - u).astype(o_ref.dtype)


def _pool_group_linear(a, w_grp, uz, scale):
    bsz, t_all, e = a.shape
    ng, gw, _ = w_grp.shape
    tm = _tile(t_all, 1088, 16)
    tn = _tile(gw, 1024, LANES)
    npg = gw // tn

    def epilogue(acc, pids, z_ref, sc_ref):
        return acc * sc_ref[...] * _silu(z_ref[...].astype(_F32))

    extras = [uz, scale.reshape(1, e)]
    especs = [pl.BlockSpec((None, tm, tn), lambda b, g, j, i, k: (b, i, (e // tn) + g * npg + j)),
              pl.BlockSpec((1, tn), lambda b, g, j, i, k: (0, g * npg + j))]
    return _matmul(
        (bsz, ng, npg, t_all // tm, 1),
        a, pl.BlockSpec((None, tm, gw), lambda b, g, j, i, k: (b, i, g)),
        w_grp, pl.BlockSpec((None, gw, tn), lambda b, g, j, i, k: (g, 0, j)),
        jax.ShapeDtypeStruct((bsz, t_all, e), _BF16),
        pl.BlockSpec((None, tm, tn), lambda b, g, j, i, k: (b, i, g * npg + j)),
        extras=tuple(extras), extra_specs=tuple(especs), epilogue=epilogue, name="pool_group")


def _pool_branch(h, w_in, w_grp, scale, seq):
    bsz, t_all, _ = h.shape
    uz = _in_proj(h, w_in)
    e = uz.shape[-1] // 2
    te = _tile(e // N_GROUPS, 256, LANES)
    tok = pl.BlockSpec((None, t_all, te), lambda b, j: (b, 0, j))
    dlt = pl.pallas_call(
        functools.partial(_pool_kernel, seq=seq, n_ctx=t_all - seq, tiles_per_group=e // N_GROUPS // te),
        grid=(bsz, e // te),
        in_specs=[tok],
        out_specs=tok,
        out_shape=jax.ShapeDtypeStruct((bsz, t_all, e), _BF16),
        compiler_params=_cparams("parallel", "parallel"),
        name="pool",
    )(uz)
    return _pool_group_linear(dlt, w_grp, uz, scale)


def _cos_sin(n_out, n_in, period):
    ang = (np.outer(np.arange(n_out), np.arange(n_in)) % period) * (2.0 * np.pi / period)
    return np.cos(ang), np.sin(ang)


@functools.lru_cache(maxsize=None)
def _chan_table(gw):
    c, s = _cos_sin(gw, gw, gw)
    return np.concatenate([c, s], axis=0) * gw ** -0.5


@functools.lru_cache(maxsize=None)
def _time_tables(seq, n_ctx):
    n1 = max(f for f in range(1, math.isqrt(seq) + 1) if seq % f == 0)
    n2 = seq // n1
    k1 = np.arange(n1)[:, None]
    stage1 = []
    for t2 in range(n2):
        ang = ((k1 * (n2 * np.arange(n1)[None, :] + t2)) % seq) * (2.0 * np.pi / seq)
        c, s = np.cos(ang), np.sin(ang)
        stage1.append(np.block([[c, -s], [-s, -c]]))
    stage1 = np.stack(stage1) * seq ** -0.5
    c2, s2 = _cos_sin(n2, n2, n2)
    stage2 = np.concatenate([c2, s2], axis=1)
    cc, sc = _cos_sin(n_ctx, n_ctx, n_ctx)
    ctx_tab = np.concatenate([cc, -sc], axis=1) * n_ctx ** -0.5
    return n1, n2, stage1, stage2, ctx_tab


def _time_dft_kernel(y1_ref, y2_ref, z_ref, w1_ref, w2_ref, wc_ref, o_ref, y1f, y2f, ar, ai, *, seq, n1, n2):
    slabs = [slice(s * LANES, (s + 1) * LANES) for s in range(y1f.shape[0])]

    def gather(buf, rows):
        return jnp.concatenate([buf[s, rows, :] for s in range(len(slabs))], axis=1).astype(_BF16)

    for s, cols in enumerate(slabs):
        y1f[s] = y1_ref[:seq, cols].astype(_F32)
        y2f[s] = y2_ref[:seq, cols].astype(_F32)
    for t2 in range(n2):
        rows = pl.ds(t2, n1, stride=n2)
        rhs = jnp.concatenate([gather(y1f, rows), gather(y2f, rows)], axis=0)
        res = jnp.dot(w1_ref[t2], rhs, preferred_element_type=_F32)
        for s, cols in enumerate(slabs):
            ar[s, t2 * n1:(t2 + 1) * n1, :] = res[:n1, cols]
            ai[s, t2 * n1:(t2 + 1) * n1, :] = res[n1:, cols]
    out = y1f
    for k1 in range(n1):
        rows = pl.ds(k1, n2, stride=n1)
        rhs = jnp.concatenate([gather(ar, rows), gather(ai, rows)], axis=0)
        res = jnp.dot(w2_ref[...], rhs, preferred_element_type=_F32)
        for s, cols in enumerate(slabs):
            out[s, rows, :] = res[:, cols]
    for s, cols in enumerate(slabs):
        o_ref[:seq, cols] = (out[s] * _silu(z_ref[:seq, cols].astype(_F32))).astype(o_ref.dtype)
    rhs = jnp.concatenate([y1_ref[seq:, :], y2_ref[seq:, :]], axis=0)
    res = jnp.dot(wc_ref[...], rhs, preferred_element_type=_F32)
    o_ref[seq:, :] = (res * _silu(z_ref[seq:, :].astype(_F32))).astype(o_ref.dtype)


def _fourier_branch(h, w_in, w_grp, seq):
    bsz, t_all, _ = h.shape
    n_ctx = t_all - seq
    uz = _in_proj(h, w_in)
    e = uz.shape[-1] // 2
    gw = e // N_GROUPS
    tab = jnp.asarray(_chan_table(gw), dtype=_BF16)
    tw = _tile(gw, 1024, LANES)
    nrow = gw // tw
    wcomb = _matmul(
        (N_GROUPS, gw // tw, 2 * nrow, 1),
        tab, pl.BlockSpec((tw, gw), lambda g, j, i, k: (i, 0)),
        w_grp, pl.BlockSpec((None, gw, tw), lambda g, j, i, k: (g, 0, j)),
        jax.ShapeDtypeStruct((N_GROUPS, gw, 2 * gw), _BF16),
        pl.BlockSpec((None, tw, tw), lambda g, j, i, k: (g, i % nrow, (i // nrow) * (gw // tw) + j)),
        name="fourier_weight")
    tm = _tile(t_all, 1088, 16)
    tn = _tile(gw, 1024, LANES)
    y = _matmul(
        (bsz, N_GROUPS, 2 * gw // tn, t_all // tm, 1),
        uz, pl.BlockSpec((None, tm, gw), lambda b, g, j, i, k: (b, i, g)),
        wcomb, pl.BlockSpec((None, gw, tn), lambda b, g, j, i, k: (g, 0, j)),
        jax.ShapeDtypeStruct((bsz, t_all, 2 * e), _BF16),
        pl.BlockSpec((None, tm, tn), lambda b, g, j, i, k: (b, i, g * (2 * gw // tn) + j)),
        name="dft_chan")
    n1, n2, stage1, stage2, ctx_tab = _time_tables(seq, n_ctx)
    tc = _tile(gw, 256, LANES)
    cpg = gw // tc

    def ycol(part):
        return lambda b, c: (b, 0, (2 * (c // cpg) + part) * cpg + c % cpg)

    tok = lambda imap: pl.BlockSpec((None, t_all, tc), imap)
    full = lambda a: pl.BlockSpec(a.shape, lambda b, c: (0,) * a.ndim)
    w1 = jnp.asarray(stage1, dtype=_BF16)
    w2 = jnp.asarray(stage2, dtype=_BF16)
    wc = jnp.asarray(ctx_tab, dtype=_BF16)
    return pl.pallas_call(
        functools.partial(_time_dft_kernel, seq=seq, n1=n1, n2=n2),
        grid=(bsz, e // tc),
        in_specs=[tok(ycol(0)), tok(ycol(1)), tok(lambda b, c: (b, 0, e // tc + c)),
                  full(w1), full(w2), full(wc)],
        out_specs=tok(lambda b, c: (b, 0, c)),
        out_shape=jax.ShapeDtypeStruct((bsz, t_all, e), _BF16),
        scratch_shapes=[pltpu.VMEM((tc // LANES, seq, LANES), _F32) for _ in range(4)],
        compiler_params=_cparams("parallel", "parallel"),
        name="dft_time",
    )(y, y, uz, w1, w2, wc)


def kernel(x, c, ctx, c_ctx, ada_w, ada_b, norm_g, final_g, a_w_in, a_conv_w, a_conv_b, a_wq, a_wk, a_wv,
           a_w_ig, a_b_ig, a_w_fg, a_b_fg, a_hnorm_w, a_skip, a_w_out, b_w_in, b_w_grp, b_scale, b_w_out,
           c_w_in, c_w_grp, c_w_out):
    bsz, seq, d = x.shape
    depth = ada_w.shape[0]
    n_rows = 8
    assert bsz + 1 <= n_rows
    rows = jnp.concatenate([c, c_ctx[None], jnp.zeros((n_rows - bsz - 1, d), c.dtype)], axis=0)
    mods = _ada(rows, ada_w, ada_b).reshape(depth, n_rows, 1, 3 * d)
    xs = _embed(x, ctx)
    for i in range(depth):
        kind = i % N_MIXERS
        j = i // N_MIXERS
        h = _prenorm(xs, norm_g[i], mods[i], seq)
        if kind == 0:
            a = _mlstm_branch(h, a_w_in[j], a_conv_w[j], a_conv_b[j], a_wq[j], a_wk[j], a_wv[j],
                              a_w_ig[j], a_b_ig[j], a_w_fg[j], a_b_fg[j], a_hnorm_w[j], a_skip[j], seq)
            w_out = a_w_out[j]
        elif kind == 1:
            a = _pool_branch(h, b_w_in[j], b_w_grp[j], b_scale[j], seq)
            w_out = b_w_out[j]
        else:
            a = _fourier_branch(h, c_w_in[j], c_w_grp[j], seq)
            w_out = c_w_out[j]
        xs = _out_proj(a, w_out, xs, mods[i], seq)
    return _final_norm(xs, final_g, seq)
```

```python
import functools
import math

import jax
import jax.numpy as jnp
import numpy as np
from jax import lax
from jax.experimental import pallas as pl
from jax.experimental.pallas import tpu as pltpu

GRID_W = 64
N_MIXERS = 3
MLSTM_HEADS = 8
QKV_BLOCK = 4
CONV_W = 4
POOL_WINDOWS = (2, 4, 8, 16)
N_GROUPS = 4
POS_BASE = 10000.0
EPS = 1e-6

LANES = 128
SUBLANES = 8
VMEM_LIMIT_BYTES = 56 * 1024 * 1024
SCAN_CHUNK = 256
SCAN_HEADS_PER_STEP = 2

_F32 = jnp.float32
_BF16 = jnp.bfloat16


def _cparams(*sem):
    return pltpu.CompilerParams(dimension_semantics=sem, vmem_limit_bytes=VMEM_LIMIT_BYTES)


def _tile(n, target, align):
    best = None
    for t in range(align, min(n, target) + 1, align):
        if n % t == 0:
            best = t
    return n if best is None else best


def _silu(v):
    return v * (1.0 / (1.0 + jnp.exp(-v)))


def _matmul(grid, a, a_spec, b, b_spec, out_shape, out_spec, extras=(), extra_specs=(), epilogue=None,
            name="matmul", first_visit=None):
    nk = grid[-1]
    n_extra = len(extras)
    k_axis = len(grid) - 1
    if epilogue is None:
        epilogue = lambda acc, pids: acc
    tm, tn = out_spec.block_shape[-2], out_spec.block_shape[-1]
    tk = b_spec.block_shape[-2]

    def body(*refs):
        a_ref, b_ref = refs[0], refs[1]
        ex = refs[2:2 + n_extra]
        o_ref = refs[2 + n_extra]
        pids = tuple(pl.program_id(ax) for ax in range(len(grid)))
        kk = pids[k_axis]
        if first_visit is None:
            bv = b_ref[...].astype(_BF16)
        else:
            w_bf = refs[-1]

            @pl.when(first_visit(pids))
            def _():
                w_bf[kk] = b_ref[...].astype(_BF16)

            bv = w_bf[kk]
        part = jnp.dot(a_ref[...].astype(_BF16), bv, preferred_element_type=_F32)

        def finish(acc):
            o_ref[...] = epilogue(acc, pids, *ex).astype(o_ref.dtype)

        if nk == 1:
            finish(part)
            return
        acc_ref = refs[3 + n_extra]

        @pl.when(kk == 0)
        def _():
            acc_ref[...] = part

        @pl.when(jnp.logical_and(kk > 0, kk < nk - 1))
        def _():
            acc_ref[...] += part

        @pl.when(kk == nk - 1)
        def _():
            finish(acc_ref[...] + part)

    scratch = [pltpu.VMEM((tm, tn), _F32)] if nk > 1 else []
    if first_visit is None:
        sem = ("parallel",) * k_axis + ("arbitrary",)
    else:
        scratch.append(pltpu.VMEM((nk, tk, tn), _BF16))
        sem = ("parallel",) + ("arbitrary",) * k_axis
    return pl.pallas_call(
        body,
        grid=grid,
        in_specs=[a_spec, b_spec, *extra_specs],
        out_specs=out_spec,
        out_shape=out_shape,
        scratch_shapes=scratch,
        compiler_params=_cparams(*sem),
        name=name,
    )(a, b, *extras)


def _ada_kernel(s_ref, w_ref, b_ref, o_ref):
    s = _silu(s_ref[...]).astype(_BF16)
    acc = jnp.dot(s, w_ref[...].astype(_BF16), preferred_element_type=_F32)
    o_ref[...] = acc + b_ref[...]


def _ada(rows, ada_w, ada_b):
    depth, d, n3 = ada_w.shape
    nr = rows.shape[0]
    tn = _tile(n3, 512, LANES)
    return pl.pallas_call(
        _ada_kernel,
        grid=(depth, n3 // tn),
        in_specs=[
            pl.BlockSpec((nr, d), lambda i, n: (0, 0)),
            pl.BlockSpec((None, d, tn), lambda i, n: (i, 0, n)),
            pl.BlockSpec((None, 1, tn), lambda i, n: (i, 0, n)),
        ],
        out_specs=pl.BlockSpec((None, nr, tn), lambda i, n: (i, 0, n)),
        out_shape=jax.ShapeDtypeStruct((depth, nr, n3), _F32),
        compiler_params=_cparams("parallel", "parallel"),
        name="ada",
    )(rows, ada_w, ada_b.reshape(depth, 1, n3))


def _embed_kernel(x_ref, c_ref, re_ref, ce_ref, o_ref, *, n_rows, half):
    r = pl.program_id(1)

    @pl.when(r < n_rows)
    def _():
        o_ref[:, :half] = x_ref[:, :half] + re_ref[...]
        o_ref[:, half:] = x_ref[:, half:] + ce_ref[...]

    @pl.when(r >= n_rows)
    def _():
        o_ref[...] = c_ref[...]


def _embed(x, ctx):
    bsz, seq, d = x.shape
    n_ctx = ctx.shape[1]
    n_rows = seq // GRID_W
    n_cblk = n_ctx // GRID_W
    half = d // 2
    quarter = d // 4
    omega = 1.0 / (POS_BASE ** (jnp.arange(quarter, dtype=_F32) / quarter))

    def axis_emb(p):
        ang = p[:, None] * omega[None, :]
        return jnp.concatenate([jnp.sin(ang), jnp.cos(ang)], axis=-1)

    row_emb = axis_emb(jnp.arange(n_rows, dtype=_F32)).reshape(n_rows, 1, half)
    col_emb = axis_emb(jnp.arange(GRID_W, dtype=_F32))
    return pl.pallas_call(
        functools.partial(_embed_kernel, n_rows=n_rows, half=half),
        grid=(bsz, n_rows + n_cblk),
        in_specs=[
            pl.BlockSpec((None, GRID_W, d), lambda b, r: (b, jnp.minimum(r, n_rows - 1), 0)),
            pl.BlockSpec((None, GRID_W, d), lambda b, r: (b, jnp.maximum(r - n_rows, 0), 0)),
            pl.BlockSpec((None, 1, half), lambda b, r: (jnp.minimum(r, n_rows - 1), 0, 0)),
            pl.BlockSpec((GRID_W, half), lambda b, r: (0, 0)),
        ],
        out_specs=pl.BlockSpec((None, GRID_W, d), lambda b, r: (b, r, 0)),
        out_shape=jax.ShapeDtypeStruct((bsz, seq + n_ctx, d), x.dtype),
        compiler_params=_cparams("parallel", "parallel"),
        name="embed",
    )(x, ctx, row_emb, col_emb)


def _prenorm_kernel(x_ref, g_ref, sh_ref, sc_ref, o_ref):
    x = x_ref[...]
    y = x * lax.rsqrt(jnp.mean(x * x, axis=-1, keepdims=True) + EPS)
    o_ref[...] = (y * g_ref[...] * (1.0 + sc_ref[...]) + sh_ref[...]).astype(o_ref.dtype)


def _prenorm(xs, g, mods, seq):
    bsz, t_all, d = xs.shape
    tb = _tile(math.gcd(seq, t_all - seq), 256, 8)
    n_lat = seq // tb

    def mrow(b, j):
        return jnp.where(j < n_lat, b, bsz)

    return pl.pallas_call(
        _prenorm_kernel,
        grid=(bsz, t_all // tb),
        in_specs=[
            pl.BlockSpec((None, tb, d), lambda b, j: (b, j, 0)),
            pl.BlockSpec((1, d), lambda b, j: (0, 0)),
            pl.BlockSpec((None, 1, d), lambda b, j: (mrow(b, j), 0, 0)),
            pl.BlockSpec((None, 1, d), lambda b, j: (mrow(b, j), 0, 1)),
        ],
        out_specs=pl.BlockSpec((None, tb, d), lambda b, j: (b, j, 0)),
        out_shape=jax.ShapeDtypeStruct((bsz, t_all, d), _BF16),
        compiler_params=_cparams("parallel", "parallel"),
        name="prenorm",
    )(xs, g.reshape(1, d), mods, mods)


def _final_norm_kernel(x_ref, g_ref, o_ref):
    x = x_ref[...]
    y = x * lax.rsqrt(jnp.mean(x * x, axis=-1, keepdims=True) + EPS)
    o_ref[...] = y * g_ref[...]


def _final_norm(xs, g, seq):
    bsz, t_all, d = xs.shape
    tb = _tile(math.gcd(seq, t_all - seq), 256, 8)
    return pl.pallas_call(
        _final_norm_kernel,
        grid=(bsz, seq // tb),
        in_specs=[
            pl.BlockSpec((None, tb, d), lambda b, j: (b, j, 0)),
            pl.BlockSpec((1, d), lambda b, j: (0, 0)),
        ],
        out_specs=pl.BlockSpec((None, tb, d), lambda b, j: (b, j, 0)),
        out_shape=jax.ShapeDtypeStruct((bsz, seq, d), xs.dtype),
        compiler_params=_cparams("parallel", "parallel"),
        name="final_norm",
    )(xs, g.reshape(1, d))


def _weight_block(layer, tk, tn, nk):
    def imap(j, b, i, k):
        first = jnp.logical_and(b == 0, i == 0)
        return (layer, jnp.where(first, k, nk - 1), j)
    return pl.BlockSpec((None, tk, tn), imap)


def _first_visit(pids):
    return jnp.logical_and(pids[1] == 0, pids[2] == 0)


def _in_proj(h, w, layer):
    bsz, t_all, d = h.shape
    n = w.shape[2]
    tm = _tile(t_all, 1088, 16)
    tn = _tile(n, 1024, LANES)
    tk = _tile(d, 2048, LANES)
    nk = d // tk
    return _matmul(
        (n // tn, bsz, t_all // tm, nk),
        h, pl.BlockSpec((None, tm, tk), lambda j, b, i, k: (b, i, k)),
        w, _weight_block(layer, tk, tn, nk),
        jax.ShapeDtypeStruct((bsz, t_all, n), _BF16),
        pl.BlockSpec((None, tm, tn), lambda j, b, i, k: (b, i, j)),
        name="in_proj", first_visit=_first_visit)


def _out_proj(a, w, layer, xs, mods, seq):
    bsz, t_all, e = a.shape
    d = w.shape[2]
    tm = _tile(t_all, 1088, 16)
    tn = _tile(d, 512, LANES)
    tk = _tile(e, 2048, LANES)
    nk = e // tk
    g_off = 2 * (d // tn)

    def epilogue(acc, pids, res_ref, gl_ref, gc_ref):
        row = pids[2] * tm + lax.broadcasted_iota(jnp.int32, acc.shape, 0)
        gate = jnp.where(row < seq, gl_ref[...], gc_ref[...])
        return res_ref[...] + gate * acc

    return _matmul(
        (d // tn, bsz, t_all // tm, nk),
        a, pl.BlockSpec((None, tm, tk), lambda j, b, i, k: (b, i, k)),
        w, _weight_block(layer, tk, tn, nk),
        jax.ShapeDtypeStruct((bsz, t_all, d), xs.dtype),
        pl.BlockSpec((None, tm, tn), lambda j, b, i, k: (b, i, j)),
        extras=(xs, mods, mods),
        extra_specs=(
            pl.BlockSpec((None, tm, tn), lambda j, b, i, k: (b, i, j)),
            pl.BlockSpec((None, 1, tn), lambda j, b, i, k: (b, 0, g_off + j)),
            pl.BlockSpec((None, 1, tn), lambda j, b, i, k: (bsz, 0, g_off + j)),
        ),
        epilogue=epilogue, name="out_proj", first_visit=_first_visit)


def _segment_pos(shape, seq, n_ctx):
    t = lax.broadcasted_iota(jnp.int32, shape, 0)
    is_lat = t < seq
    return jnp.where(is_lat, t, t - seq), jnp.where(is_lat, seq, n_ctx)


def _shift_rows(u, k, pos, seg_len):
    t_all = u.shape[0]
    rolled = pltpu.roll(u, (-k) % t_all, axis=0)
    valid = (pos >= -k) if k < 0 else (pos < seg_len - k)
    return jnp.where(valid, rolled, 0.0)


def _mlstm_pre_kernel(u_ref, cw_ref, cb_ref, wq_ref, wk_ref, wv_ref, wg_ref, gb_ref,
                      xc_ref, q_ref, k_ref, v_ref, g_ref, *, seq, n_ctx, k_scale, n_gate):
    j = pl.program_id(1)
    ub = u_ref[...]
    u = ub.astype(_F32)
    pos, seg_len = _segment_pos(u.shape, seq, n_ctx)
    left = (CONV_W - 1) // 2
    acc = cb_ref[...] + u * cw_ref[left:left + 1, :]
    for jj in range(CONV_W):
        if jj != left:
            acc = acc + _shift_rows(u, jj - left, pos, seg_len) * cw_ref[jj:jj + 1, :]
    xc = _silu(acc)
    xcb = xc.astype(_BF16)
    q = jnp.dot(xcb, wq_ref[...], preferred_element_type=_F32)
    k = jnp.dot(xcb, wk_ref[...], preferred_element_type=_F32)
    v = jnp.dot(ub, wv_ref[...], preferred_element_type=_F32)
    qb, kb, vb = q.astype(_BF16), k.astype(_BF16), v.astype(_BF16)
    xc_ref[...] = xcb
    q_ref[...] = qb
    k_ref[...] = (k * k_scale).astype(_BF16)
    v_ref[...] = vb
    part = jnp.dot(jnp.concatenate([qb, kb, vb], axis=1), wg_ref[...], preferred_element_type=_F32)

    @pl.when(j == 0)
    def _():
        g_ref[...] = part + gb_ref[...]

    @pl.when(j > 0)
    def _():
        g_ref[...] += part

    @pl.when(j == pl.num_programs(1) - 1)
    def _():
        g = g_ref[...]
        col = lax.broadcasted_iota(jnp.int32, g.shape, 1)
        log_sig = jnp.minimum(g, 0.0) - jnp.log1p(jnp.exp(-jnp.abs(g)))
        g_ref[...] = jnp.where(col < n_gate, g, log_sig)


def _blockdiag_tiles(w, te):
    nb = w.shape[0]
    per = te // QKV_BLOCK
    wt = w.reshape(nb // per, per, QKV_BLOCK, QKV_BLOCK)
    eye = jnp.eye(per, dtype=w.dtype)
    dense = jnp.einsum('tgio,gh->tgiho', wt, eye)
    return dense.reshape(nb // per, te, te).astype(_BF16)


def _mlstm_pre(uz, conv_w, conv_b, wq, wk, wv, w_ig, b_ig, w_fg, b_fg, seq):
    bsz, t_all, e2 = uz.shape
    e = e2 // 2
    nh = w_ig.shape[-1]
    te = LANES
    nt = e // te
    ng = 4 * nh
    wg = jnp.concatenate([w_ig[0], w_ig[1], w_fg[0], w_fg[1]], axis=-1)
    wg = wg.reshape(3, nt, te, ng).transpose(1, 0, 2, 3).reshape(nt, 3 * te, ng).astype(_BF16)
    gb = jnp.concatenate([b_ig[0], b_ig[1], b_fg[0], b_fg[1]], axis=-1).reshape(1, ng)
    tok = pl.BlockSpec((None, t_all, te), lambda b, j: (b, 0, j))
    bd = pl.BlockSpec((None, te, te), lambda b, j: (j, 0, 0))
    act = jax.ShapeDtypeStruct((bsz, t_all, e), _BF16)
    return pl.pallas_call(
        functools.partial(_mlstm_pre_kernel, seq=seq, n_ctx=t_all - seq,
                          k_scale=float((e // nh) ** -0.5), n_gate=2 * nh),
        grid=(bsz, nt),
        in_specs=[
            tok,
            pl.BlockSpec((CONV_W, te), lambda b, j: (0, j)),
            pl.BlockSpec((1, te), lambda b, j: (0, j)),
            bd, bd, bd,
            pl.BlockSpec((None, 3 * te, ng), lambda b, j: (j, 0, 0)),
            pl.BlockSpec((1, ng), lambda b, j: (0, 0)),
        ],
        out_specs=[tok, tok, tok, tok, pl.BlockSpec((None, t_all, ng), lambda b, j: (b, 0, 0))],
        out_shape=[act, act, act, act, jax.ShapeDtypeStruct((bsz, t_all, ng), _F32)],
        compiler_params=_cparams("parallel", "arbitrary"),
        name="mlstm_pre",
    )(uz, conv_w, conv_b.reshape(1, e), _blockdiag_tiles(wq, te), _blockdiag_tiles(wk, te),
      _blockdiag_tiles(wv, te), wg, gb)


def _scan_kernel(q_ref, k_ref, v_ref, li_ref, lf_ref, h_ref, c_sc, n_sc, m_sc):
    rev = pl.program_id(0)
    step = pl.program_id(3)

    @pl.when(step == 0)
    def _():
        c_sc[...] = jnp.zeros(c_sc.shape, _F32)
        n_sc[...] = jnp.zeros(n_sc.shape, _F32)
        m_sc[...] = jnp.zeros(m_sc.shape, _F32)

    ln = q_ref.shape[0]
    dh = c_sc.shape[-1]
    ti = lax.broadcasted_iota(jnp.int32, (ln, ln), 0)
    si = lax.broadcasted_iota(jnp.int32, (ln, ln), 1)
    sgn = 1 - 2 * rev
    incl = (si - ti) * sgn <= 0
    incl_t = (ti - si) * sgn <= 0
    eye = si == ti
    for hh in range(c_sc.shape[0]):
        cols = slice(hh * dh, (hh + 1) * dh)
        h_ref[:, cols] = _scan_chunk(q_ref[:, cols], k_ref[:, cols], v_ref[:, cols], li_ref[hh], lf_ref[hh],
                                     incl, incl_t, eye, c_sc.at[hh], n_sc.at[hh], m_sc.at[hh]).astype(h_ref.dtype)


def _scan_chunk(q, k, v, li_r, lf_r, incl, incl_t, eye, c_sc, n_sc, m_sc):
    ln = q.shape[0]
    lf_b = jnp.broadcast_to(lf_r, (ln, ln))
    li_b = jnp.broadcast_to(li_r, (ln, ln))
    b_col = jnp.sum(jnp.where(incl, lf_b, 0.0), axis=1, keepdims=True)
    lf_col = jnp.sum(jnp.where(eye, lf_b, 0.0), axis=1, keepdims=True)
    li_col = jnp.sum(jnp.where(eye, li_b, 0.0), axis=1, keepdims=True)
    b_row = jnp.sum(jnp.where(incl_t, lf_col, 0.0), axis=0, keepdims=True)
    b_end = jnp.sum(lf_r, axis=1, keepdims=True)
    m_old = m_sc[...]
    g_row = b_end - b_row + li_r
    g_col = b_end - b_col + li_col
    m_new = jnp.maximum(b_end + m_old, jnp.max(g_row, axis=1, keepdims=True))
    wg_col = jnp.exp(g_col - m_new)
    decay = jnp.exp(b_end + m_old - m_new)

    logw = jnp.where(incl, b_col - b_row + li_r, -jnp.inf)
    inter = b_col + m_old
    m_t = jnp.maximum(jnp.max(logw, axis=1, keepdims=True), inter)
    s = lax.dot_general(q, k, (((1,), (1,)), ((), ())), preferred_element_type=_F32) * jnp.exp(logw - m_t)
    w_inter = jnp.exp(inter - m_t)
    c_old = c_sc[...]
    num = (jnp.dot(s.astype(_BF16), v, preferred_element_type=_F32)
           + w_inter * jnp.dot(q, c_old.astype(_BF16), preferred_element_type=_F32))
    den = (jnp.sum(s, axis=1, keepdims=True)
           + w_inter * jnp.sum(q.astype(_F32) * n_sc[...], axis=1, keepdims=True))
    h = num / jnp.maximum(jnp.abs(den), jnp.exp(-m_t))

    vw = (v.astype(_F32) * wg_col).astype(_BF16)
    c_sc[...] = decay * c_old + lax.dot_general(k, vw, (((0,), (0,)), ((), ())),
                                                preferred_element_type=_F32)
    n_sc[...] = decay * n_sc[...] + jnp.sum(k.astype(_F32) * wg_col, axis=0, keepdims=True)
    m_sc[...] = m_new
    return h


def _mlstm_scan(q, k, v, gates, seq, nh):
    bsz, t_all, e = q.shape
    dh = e // nh
    n_ctx = t_all - seq
    ln = _tile(math.gcd(seq, n_ctx), SCAN_CHUNK, 8)
    nc = t_all // ln
    ncl = seq // ln
    ncc = n_ctx // ln
    g = gates.reshape(bsz, nc, ln, 2, 2, nh).transpose(3, 4, 0, 5, 1, 2).reshape(2, 2, bsz, nh, nc, 1, ln)

    def chunk(d, s):
        fwd = jnp.where(s < ncc, ncl + s, s - ncc)
        bwd = jnp.where(s < ncc, ncl + ncc - 1 - s, ncl - 1 - (s - ncc))
        return jnp.where(d == 0, fwd, bwd)

    hb = SCAN_HEADS_PER_STEP if nh % SCAN_HEADS_PER_STEP == 0 else 1
    tok = pl.BlockSpec((None, ln, hb * dh), lambda d, b, hh, s: (b, chunk(d, s), hh))
    gate = pl.BlockSpec((None, None, hb, None, 1, ln), lambda d, b, hh, s: (d, b, hh, chunk(d, s), 0, 0))
    return pl.pallas_call(
        _scan_kernel,
        grid=(2, bsz, nh // hb, nc),
        in_specs=[tok, tok, tok, gate, gate],
        out_specs=pl.BlockSpec((None, None, ln, hb * dh), lambda d, b, hh, s: (d, b, chunk(d, s), hh)),
        out_shape=jax.ShapeDtypeStruct((2, bsz, t_all, e), _BF16),
        scratch_shapes=[pltpu.VMEM((hb, dh, dh), _F32), pltpu.VMEM((hb, 1, dh), _F32),
                        pltpu.VMEM((hb, 1, 1), _F32)],
        compiler_params=_cparams("parallel", "parallel", "parallel", "arbitrary"),
        name="mlstm_scan",
    )(q, k, v, g[0], g[1])


def _mlstm_finish_kernel(hf_ref, hr_ref, xc_ref, z_ref, nw_ref, sk_ref, o_ref):
    h = hf_ref[...].astype(_F32) + hr_ref[...].astype(_F32)
    mu = jnp.mean(h, axis=-1, keepdims=True)
    hc = h - mu
    var = jnp.mean(hc * hc, axis=-1, keepdims=True)
    hn = hc * lax.rsqrt(var + EPS) * nw_ref[...]
    y = (hn + sk_ref[...] * xc_ref[...].astype(_F32)) * _silu(z_ref[...].astype(_F32))
    o_ref[...] = y.astype(o_ref.dtype)


def _mlstm_finish(h2, xc, uz, hnorm_w, skip, nh):
    bsz, t_all, e = xc.shape
    dh = e // nh
    tb = _tile(t_all, 544, 16)
    hspec = lambda d: pl.BlockSpec((None, None, tb, dh), lambda b, i, hh: (d, b, i, hh))
    tok = pl.BlockSpec((None, tb, dh), lambda b, i, hh: (b, i, hh))
    vec = pl.BlockSpec((1, dh), lambda b, i, hh: (0, hh))
    return pl.pallas_call(
        _mlstm_finish_kernel,
        grid=(bsz, t_all // tb, nh),
        in_specs=[hspec(0), hspec(1), tok,
                  pl.BlockSpec((None, tb, dh), lambda b, i, hh: (b, i, nh + hh)), vec, vec],
        out_specs=tok,
        out_shape=jax.ShapeDtypeStruct((bsz, t_all, e), _BF16),
        compiler_params=_cparams("parallel", "parallel", "parallel"),
        name="mlstm_finish",
    )(h2, h2, xc, uz, hnorm_w.reshape(1, e), skip.reshape(1, e))


def _mlstm_branch(uz, conv_w, conv_b, wq, wk, wv, w_ig, b_ig, w_fg, b_fg, hnorm_w, skip, seq):
    nh = w_ig.shape[-1]
    xc, q, k, v, gates = _mlstm_pre(uz, conv_w, conv_b, wq, wk, wv, w_ig, b_ig, w_fg, b_fg, seq)
    h2 = _mlstm_scan(q, k, v, gates, seq, nh)
    return _mlstm_finish(h2, xc, uz, hnorm_w, skip, nh)


def _pool_kernel(u_ref, o_ref, *, seq, n_ctx, tiles_per_group):
    grp = pl.program_id(1) // tiles_per_group
    u = u_ref[...].astype(_F32)
    pos, seg_len = _segment_pos(u.shape, seq, n_ctx)
    for gi, w in enumerate(POOL_WINDOWS):
        @pl.when(grp == gi)
        def _(w=w):
            lo = w // 2
            hi = w - 1 - lo
            trail = u
            lead = u
            span = 1
            while span < lo:
                trail = trail + _shift_rows(trail, -span, pos, seg_len)
                lead = lead + _shift_rows(lead, span, pos, seg_len)
                span *= 2
            win = _shift_rows(trail, -1, pos, seg_len) + lead
            cnt = jnp.minimum(pos + hi + 1, seg_len) - jnp.maximum(pos - lo, 0)
            o_ref[...] = (win / cnt.astype(_F32) - u).astype(o_ref.dtype)


def _pool_group_linear(a, w_grp, uz, scale):
    bsz, t_all, e = a.shape
    ng, gw, _ = w_grp.shape
    tm = _tile(t_all, 1088, 16)
    tn = _tile(gw, 1024, LANES)
    npg = gw // tn

    def epilogue(acc, pids, z_ref, sc_ref):
        return acc * sc_ref[...] * _silu(z_ref[...].astype(_F32))

    extras = [uz, scale.reshape(1, e)]
    especs = [pl.BlockSpec((None, tm, tn), lambda b, g, j, i, k: (b, i, (e // tn) + g * npg + j)),
              pl.BlockSpec((1, tn), lambda b, g, j, i, k: (0, g * npg + j))]
    return _matmul(
        (bsz, ng, npg, t_all // tm, 1),
        a, pl.BlockSpec((None, tm, gw), lambda b, g, j, i, k: (b, i, g)),
        w_grp, pl.BlockSpec((None, gw, tn), lambda b, g, j, i, k: (g, 0, j)),
        jax.ShapeDtypeStruct((bsz, t_all, e), _BF16),
        pl.BlockSpec((None, tm, tn), lambda b, g, j, i, k: (b, i, g * npg + j)),
        extras=tuple(extras), extra_specs=tuple(especs), epilogue=epilogue, name="pool_group")


def _pool_branch(uz, w_grp, scale, seq):
    bsz, t_all, e2 = uz.shape
    e = e2 // 2
    te = _tile(e // N_GROUPS, 256, LANES)
    tok = pl.BlockSpec((None, t_all, te), lambda b, j: (b, 0, j))
    dlt = pl.pallas_call(
        functools.partial(_pool_kernel, seq=seq, n_ctx=t_all - seq, tiles_per_group=e // N_GROUPS // te),
        grid=(bsz, e // te),
        in_specs=[tok],
        out_specs=tok,
        out_shape=jax.ShapeDtypeStruct((bsz, t_all, e), _BF16),
        compiler_params=_cparams("parallel", "parallel"),
        name="pool",
    )(uz)
    return _pool_group_linear(dlt, w_grp, uz, scale)


def _cos_sin(n_out, n_in, period):
    ang = (np.outer(np.arange(n_out), np.arange(n_in)) % period) * (2.0 * np.pi / period)
    return np.cos(ang), np.sin(ang)


@functools.lru_cache(maxsize=None)
def _chan_table(gw):
    c, s = _cos_sin(gw, gw, gw)
    return np.concatenate([c, s], axis=0) * gw ** -0.5


@functools.lru_cache(maxsize=None)
def _time_tables(seq, n_ctx):
    n1 = max(f for f in range(1, math.isqrt(seq) + 1) if seq % f == 0)
    n2 = seq // n1
    g = SUBLANES
    assert n1 % g == 0 and n2 % g == 0
    eye = np.eye(g)
    c1, s1 = _cos_sin(n1, n1, n1)
    base = np.block([[c1, -s1], [-s1, -c1]]) * seq ** -0.5
    stage1 = np.einsum('ab,rkst->arkstb', eye, base.reshape(2, n1, 2, n1)).reshape(g * 2 * n1, 2 * n1 * g)
    twc, tws = _cos_sin(n2, n1, seq)
    c2, s2 = _cos_sin(n2, n2, n2)
    cs2 = np.concatenate([c2, s2], axis=1)
    stage2 = np.einsum('ab,krt->kartb', eye, cs2.reshape(n2, 2, n2)).reshape(n2 * g, 2 * n2 * g)
    cc, sc = _cos_sin(n_ctx, n_ctx, n_ctx)
    ctx_tab = np.concatenate([cc, -sc], axis=1) * n_ctx ** -0.5
    lanes = lambda a: np.repeat(a[:, :, None], LANES, axis=2)
    return n1, n2, stage1, lanes(twc), lanes(tws), stage2, ctx_tab


def _time_dft_kernel(y1_ref, y2_ref, z_ref, w1_ref, twc_ref, tws_ref, w2_ref, wc_ref, o_ref,
                     y1f, y2f, ar, ai, out, *, seq, n1, n2):
    g = SUBLANES
    tc = o_ref.shape[-1]
    wide = lambda tab: jnp.concatenate([tab] * (tc // LANES), axis=1)
    y1f[...] = y1_ref[:seq, :].astype(_F32).reshape(y1f.shape)
    y2f[...] = y2_ref[:seq, :].astype(_F32).reshape(y2f.shape)
    for th in range(n2 // g):
        rhs = jnp.concatenate([y1f[:, th].reshape(n1 * g, tc), y2f[:, th].reshape(n1 * g, tc)], axis=0)
        res = jnp.dot(w1_ref[...], rhs.astype(_BF16), preferred_element_type=_F32)
        for tl in range(g):
            t2 = th * g + tl
            a_r = res[tl * 2 * n1:tl * 2 * n1 + n1]
            a_i = res[tl * 2 * n1 + n1:(tl + 1) * 2 * n1]
            c, s = wide(twc_ref[t2]), wide(tws_ref[t2])
            ar[t2] = (a_r * c + a_i * s).reshape(ar.shape[1:])
            ai[t2] = (a_i * c - a_r * s).reshape(ai.shape[1:])
    for kh in range(n1 // g):
        rhs = jnp.concatenate([ar[:, kh].reshape(n2 * g, tc), ai[:, kh].reshape(n2 * g, tc)], axis=0)
        res = jnp.dot(w2_ref[...], rhs.astype(_BF16), preferred_element_type=_F32)
        out[:, kh] = res.reshape(n2, g, tc)
    gate = _silu(z_ref[:seq, :].astype(_F32))
    o_ref[:seq, :] = (out[...].reshape(seq, tc) * gate).astype(o_ref.dtype)
    rhs = jnp.concatenate([y1_ref[seq:, :], y2_ref[seq:, :]], axis=0)
    res = jnp.dot(wc_ref[...], rhs, preferred_element_type=_F32)
    o_ref[seq:, :] = (res * _silu(z_ref[seq:, :].astype(_F32))).astype(o_ref.dtype)


def _fourier_branch(uz, w_grp, seq):
    bsz, t_all, e2 = uz.shape
    n_ctx = t_all - seq
    e = e2 // 2
    gw = e // N_GROUPS
    tab = jnp.asarray(_chan_table(gw), dtype=_BF16)
    tw = _tile(gw, 1024, LANES)
    nrow = gw // tw
    wcomb = _matmul(
        (N_GROUPS, gw // tw, 2 * nrow, 1),
        tab, pl.BlockSpec((tw, gw), lambda g, j, i, k: (i, 0)),
        w_grp, pl.BlockSpec((None, gw, tw), lambda g, j, i, k: (g, 0, j)),
        jax.ShapeDtypeStruct((N_GROUPS, gw, 2 * gw), _BF16),
        pl.BlockSpec((None, tw, tw), lambda g, j, i, k: (g, i % nrow, (i // nrow) * (gw // tw) + j)),
        name="fourier_weight")
    tm = _tile(t_all, 1088, 16)
    tn = _tile(gw, 1024, LANES)
    y = _matmul(
        (bsz, N_GROUPS, 2 * gw // tn, t_all // tm, 1),
        uz, pl.BlockSpec((None, tm, gw), lambda b, g, j, i, k: (b, i, g)),
        wcomb, pl.BlockSpec((None, gw, tn), lambda b, g, j, i, k: (g, 0, j)),
        jax.ShapeDtypeStruct((bsz, t_all, 2 * e), _BF16),
        pl.BlockSpec((None, tm, tn), lambda b, g, j, i, k: (b, i, g * (2 * gw // tn) + j)),
        name="dft_chan")
    n1, n2, stage1, twc, tws, stage2, ctx_tab = _time_tables(seq, n_ctx)
    tc = _tile(gw, 256, LANES)
    cpg = gw // tc

    def ycol(part):
        return lambda b, c: (b, 0, (2 * (c // cpg) + part) * cpg + c % cpg)

    tok = lambda imap: pl.BlockSpec((None, t_all, tc), imap)
    full = lambda a: pl.BlockSpec(a.shape, lambda b, c: (0,) * a.ndim, pipeline_mode=pl.Buffered(1))
    tabs = [jnp.asarray(stage1, dtype=_BF16), jnp.asarray(twc, dtype=_F32), jnp.asarray(tws, dtype=_F32),
            jnp.asarray(stage2, dtype=_BF16), jnp.asarray(ctx_tab, dtype=_BF16)]
    by_t1 = pltpu.VMEM((n1, n2 // SUBLANES, SUBLANES, tc), _F32)
    by_t2 = pltpu.VMEM((n2, n1 // SUBLANES, SUBLANES, tc), _F32)
    return pl.pallas_call(
        functools.partial(_time_dft_kernel, seq=seq, n1=n1, n2=n2),
        grid=(bsz, e // tc),
        in_specs=[tok(ycol(0)), tok(ycol(1)), tok(lambda b, c: (b, 0, e // tc + c))] + [full(t) for t in tabs],
        out_specs=tok(lambda b, c: (b, 0, c)),
        out_shape=jax.ShapeDtypeStruct((bsz, t_all, e), _BF16),
        scratch_shapes=[by_t1, by_t1, by_t2, by_t2, by_t2],
        compiler_params=_cparams("parallel", "parallel"),
        name="dft_time",
    )(y, y, uz, *tabs)


def kernel(x, c, ctx, c_ctx, ada_w, ada_b, norm_g, final_g, a_w_in, a_conv_w, a_conv_b, a_wq, a_wk, a_wv,
           a_w_ig, a_b_ig, a_w_fg, a_b_fg, a_hnorm_w, a_skip, a_w_out, b_w_in, b_w_grp, b_scale, b_w_out,
           c_w_in, c_w_grp, c_w_out):
    bsz, seq, d = x.shape
    depth = ada_w.shape[0]
    n_rows = 8
    assert bsz + 1 <= n_rows
    rows = jnp.concatenate([c, c_ctx[None], jnp.zeros((n_rows - bsz - 1, d), c.dtype)], axis=0)
    mods = _ada(rows, ada_w, ada_b).reshape(depth, n_rows, 1, 3 * d)
    xs = _embed(x, ctx)
    for i in range(depth):
        kind = i % N_MIXERS
        j = i // N_MIXERS
        h = _prenorm(xs, norm_g[i], mods[i], seq)
        w_in, w_out = ((a_w_in, a_w_out), (b_w_in, b_w_out), (c_w_in, c_w_out))[kind]
        uz = _in_proj(h, w_in, j)
        if kind == 0:
            a = _mlstm_branch(uz, a_conv_w[j], a_conv_b[j], a_wq[j], a_wk[j], a_wv[j],
                              a_w_ig[j], a_b_ig[j], a_w_fg[j], a_b_fg[j], a_hnorm_w[j], a_skip[j], seq)
        elif kind == 1:
            a = _pool_branch(uz, b_w_grp[j], b_scale[j], seq)
        else:
            a = _fourier_branch(uz, c_w_grp[j], seq)
        xs = _out_proj(a, w_out, j, xs, mods[i], seq)
    return _final_norm(xs, final_g, seq)
```

```python
import functools
import math

import jax
import jax.numpy as jnp
import numpy as np
from jax import lax
from jax.experimental import pallas as pl
from jax.experimental.pallas import tpu as pltpu

GRID_W = 64
N_MIXERS = 3
MLSTM_HEADS = 8
QKV_BLOCK = 4
CONV_W = 4
POOL_WINDOWS = (2, 4, 8, 16)
N_GROUPS = 4
POS_BASE = 10000.0
EPS = 1e-6

LANES = 128
SUBLANES = 8
VMEM_LIMIT_BYTES = 56 * 1024 * 1024
SCAN_CHUNK = 256
SCAN_HEADS_PER_STEP = 2

_F32 = jnp.float32
_BF16 = jnp.bfloat16


def _cparams(*sem):
    return pltpu.CompilerParams(dimension_semantics=sem, vmem_limit_bytes=VMEM_LIMIT_BYTES)


def _tile(n, target, align):
    best = None
    for t in range(align, min(n, target) + 1, align):
        if n % t == 0:
            best = t
    return n if best is None else best


def _silu(v):
    return v * (1.0 / (1.0 + jnp.exp(-v)))


def _matmul(grid, a, a_spec, b, b_spec, out_shape, out_spec, extras=(), extra_specs=(), epilogue=None,
            name="matmul"):
    n_extra = len(extras)
    if epilogue is None:
        epilogue = lambda acc, pids: acc

    def body(*refs):
        a_ref, b_ref = refs[0], refs[1]
        ex = refs[2:2 + n_extra]
        o_ref = refs[2 + n_extra]
        pids = tuple(pl.program_id(ax) for ax in range(len(grid)))
        acc = jnp.dot(a_ref[...].astype(_BF16), b_ref[...].astype(_BF16), preferred_element_type=_F32)
        o_ref[...] = epilogue(acc, pids, *ex).astype(o_ref.dtype)

    return pl.pallas_call(
        body,
        grid=grid,
        in_specs=[a_spec, b_spec, *extra_specs],
        out_specs=out_spec,
        out_shape=out_shape,
        compiler_params=_cparams(*(("parallel",) * len(grid))),
        name=name,
    )(a, b, *extras)


def _proj(a, w, layer, tm, tn, out_dtype, name, extras=(), extra_blocks=(), epilogue=None):
    bsz, t_all, k = a.shape
    n = w.shape[2]
    nt, mt = n // tn, t_all // tm
    pieces = bsz * mt
    tkp = k // pieces
    assert tkp * pieces == k and tkp % 16 == 0
    n_extra = len(extras)
    if epilogue is None:
        epilogue = lambda acc, pids: acc

    def body(*refs):
        a_ref, w_ref = refs[0], refs[1]
        ex = refs[2:2 + n_extra]
        o_ref, w_bf = refs[2 + n_extra], refs[3 + n_extra]
        pids = tuple(pl.program_id(ax) for ax in range(3))
        jj = pids[0]
        piece = pids[1] * mt + pids[2]

        def stage():
            rows = pl.ds(pl.multiple_of(piece * tkp, tkp), tkp)
            w_bf[jj % 2, rows, :] = w_ref[...].astype(_BF16)

        @pl.when(jj == 0)
        def _():
            stage()

        @pl.when(jj > 0)
        def _():
            stage()
            acc = jnp.dot(a_ref[...], w_bf[(jj + 1) % 2], preferred_element_type=_F32)
            o_ref[...] = epilogue(acc, pids, *ex).astype(o_ref.dtype)

    def a_map(jj, b, i):
        return (jnp.where(jj > 0, b, 0), jnp.where(jj > 0, i, 0), 0)

    def o_map(jj, b, i):
        return (jnp.where(jj > 0, b, 0), jnp.where(jj > 0, i, 0), jnp.maximum(jj - 1, 0))

    def w_map(jj, b, i):
        last = jj == nt
        return (layer, jnp.where(last, pieces - 1, b * mt + i), jnp.minimum(jj, nt - 1))

    return pl.pallas_call(
        body,
        grid=(nt + 1, bsz, mt),
        in_specs=[pl.BlockSpec((None, tm, k), a_map),
                  pl.BlockSpec((None, tkp, tn), w_map),
                  *[spec(o_map) for spec in extra_blocks]],
        out_specs=pl.BlockSpec((None, tm, tn), o_map),
        out_shape=jax.ShapeDtypeStruct((bsz, t_all, n), out_dtype),
        scratch_shapes=[pltpu.VMEM((2, k, tn), _BF16)],
        compiler_params=_cparams("arbitrary", "arbitrary", "arbitrary"),
        name=name,
    )(a, w, *extras)


def _ada_kernel(s_ref, w_ref, b_ref, o_ref):
    s = _silu(s_ref[...]).astype(_BF16)
    acc = jnp.dot(s, w_ref[...].astype(_BF16), preferred_element_type=_F32)
    o_ref[...] = acc + b_ref[...]


def _ada(rows, ada_w, ada_b):
    depth, d, n3 = ada_w.shape
    nr = rows.shape[0]
    tn = _tile(n3, 512, LANES)
    return pl.pallas_call(
        _ada_kernel,
        grid=(depth, n3 // tn),
        in_specs=[
            pl.BlockSpec((nr, d), lambda i, n: (0, 0)),
            pl.BlockSpec((None, d, tn), lambda i, n: (i, 0, n)),
            pl.BlockSpec((None, 1, tn), lambda i, n: (i, 0, n)),
        ],
        out_specs=pl.BlockSpec((None, nr, tn), lambda i, n: (i, 0, n)),
        out_shape=jax.ShapeDtypeStruct((depth, nr, n3), _F32),
        compiler_params=_cparams("parallel", "parallel"),
        name="ada",
    )(rows, ada_w, ada_b.reshape(depth, 1, n3))


def _embed_kernel(x_ref, c_ref, re_ref, ce_ref, o_ref, *, n_rows, half):
    r = pl.program_id(1)

    @pl.when(r < n_rows)
    def _():
        o_ref[:, :half] = x_ref[:, :half] + re_ref[...]
        o_ref[:, half:] = x_ref[:, half:] + ce_ref[...]

    @pl.when(r >= n_rows)
    def _():
        o_ref[...] = c_ref[...]


def _embed(x, ctx):
    bsz, seq, d = x.shape
    n_ctx = ctx.shape[1]
    n_rows = seq // GRID_W
    n_cblk = n_ctx // GRID_W
    half = d // 2
    quarter = d // 4
    omega = 1.0 / (POS_BASE ** (jnp.arange(quarter, dtype=_F32) / quarter))

    def axis_emb(p):
        ang = p[:, None] * omega[None, :]
        return jnp.concatenate([jnp.sin(ang), jnp.cos(ang)], axis=-1)

    row_emb = axis_emb(jnp.arange(n_rows, dtype=_F32)).reshape(n_rows, 1, half)
    col_emb = axis_emb(jnp.arange(GRID_W, dtype=_F32))
    return pl.pallas_call(
        functools.partial(_embed_kernel, n_rows=n_rows, half=half),
        grid=(bsz, n_rows + n_cblk),
        in_specs=[
            pl.BlockSpec((None, GRID_W, d), lambda b, r: (b, jnp.minimum(r, n_rows - 1), 0)),
            pl.BlockSpec((None, GRID_W, d), lambda b, r: (b, jnp.maximum(r - n_rows, 0), 0)),
            pl.BlockSpec((None, 1, half), lambda b, r: (jnp.minimum(r, n_rows - 1), 0, 0)),
            pl.BlockSpec((GRID_W, half), lambda b, r: (0, 0)),
        ],
        out_specs=pl.BlockSpec((None, GRID_W, d), lambda b, r: (b, r, 0)),
        out_shape=jax.ShapeDtypeStruct((bsz, seq + n_ctx, d), x.dtype),
        compiler_params=_cparams("parallel", "parallel"),
        name="embed",
    )(x, ctx, row_emb, col_emb)


def _prenorm_kernel(x_ref, g_ref, sh_ref, sc_ref, o_ref):
    x = x_ref[...]
    y = x * lax.rsqrt(jnp.mean(x * x, axis=-1, keepdims=True) + EPS)
    o_ref[...] = (y * g_ref[...] * (1.0 + sc_ref[...]) + sh_ref[...]).astype(o_ref.dtype)


def _prenorm(xs, g, mods, seq):
    bsz, t_all, d = xs.shape
    tb = _tile(math.gcd(seq, t_all - seq), 256, 8)
    n_lat = seq // tb

    def mrow(b, j):
        return jnp.where(j < n_lat, b, bsz)

    return pl.pallas_call(
        _prenorm_kernel,
        grid=(bsz, t_all // tb),
        in_specs=[
            pl.BlockSpec((None, tb, d), lambda b, j: (b, j, 0)),
            pl.BlockSpec((1, d), lambda b, j: (0, 0)),
            pl.BlockSpec((None, 1, d), lambda b, j: (mrow(b, j), 0, 0)),
            pl.BlockSpec((None, 1, d), lambda b, j: (mrow(b, j), 0, 1)),
        ],
        out_specs=pl.BlockSpec((None, tb, d), lambda b, j: (b, j, 0)),
        out_shape=jax.ShapeDtypeStruct((bsz, t_all, d), _BF16),
        compiler_params=_cparams("parallel", "parallel"),
        name="prenorm",
    )(xs, g.reshape(1, d), mods, mods)


def _final_norm_kernel(x_ref, g_ref, o_ref):
    x = x_ref[...]
    y = x * lax.rsqrt(jnp.mean(x * x, axis=-1, keepdims=True) + EPS)
    o_ref[...] = y * g_ref[...]


def _final_norm(xs, g, seq):
    bsz, t_all, d = xs.shape
    tb = _tile(math.gcd(seq, t_all - seq), 256, 8)
    return pl.pallas_call(
        _final_norm_kernel,
        grid=(bsz, seq // tb),
        in_specs=[
            pl.BlockSpec((None, tb, d), lambda b, j: (b, j, 0)),
            pl.BlockSpec((1, d), lambda b, j: (0, 0)),
        ],
        out_specs=pl.BlockSpec((None, tb, d), lambda b, j: (b, j, 0)),
        out_shape=jax.ShapeDtypeStruct((bsz, seq, d), xs.dtype),
        compiler_params=_cparams("parallel", "parallel"),
        name="final_norm",
    )(xs, g.reshape(1, d))


def _in_proj(h, w, layer):
    t_all, n = h.shape[1], w.shape[2]
    return _proj(h, w, layer, _tile(t_all, 1088, 16), _tile(n, 1024, LANES), _BF16, "in_proj")


def _out_proj(a, w, layer, xs, mods, seq):
    bsz, t_all, _ = a.shape
    d = w.shape[2]
    tm = _tile(t_all, 544, 16)
    tn = _tile(d, 512, LANES)
    g_off = 2 * (d // tn)

    def epilogue(acc, pids, res_ref, gl_ref, gc_ref):
        row = pids[2] * tm + lax.broadcasted_iota(jnp.int32, acc.shape, 0)
        gate = jnp.where(row < seq, gl_ref[...], gc_ref[...])
        return res_ref[...] + gate * acc

    def gate_block(row_of):
        return lambda om: pl.BlockSpec((None, 1, tn), lambda jj, b, i: (row_of(b), 0, g_off + om(jj, b, i)[2]))

    return _proj(a, w, layer, tm, tn, xs.dtype, "out_proj", extras=(xs, mods, mods),
                 extra_blocks=(lambda om: pl.BlockSpec((None, tm, tn), om),
                               gate_block(lambda b: b), gate_block(lambda b: bsz)),
                 epilogue=epilogue)


def _segment_pos(shape, seq, n_ctx):
    t = lax.broadcasted_iota(jnp.int32, shape, 0)
    is_lat = t < seq
    return jnp.where(is_lat, t, t - seq), jnp.where(is_lat, seq, n_ctx)


def _shift_rows(u, k, pos, seg_len):
    t_all = u.shape[0]
    rolled = pltpu.roll(u, (-k) % t_all, axis=0)
    valid = (pos >= -k) if k < 0 else (pos < seg_len - k)
    return jnp.where(valid, rolled, 0.0)


def _mlstm_pre_kernel(u_ref, cw_ref, cb_ref, wq_ref, wk_ref, wv_ref, wg_ref, gb_ref,
                      xc_ref, q_ref, k_ref, v_ref, g_ref, *, seq, n_ctx, k_scale, n_gate):
    j = pl.program_id(1)
    ub = u_ref[...]
    u = ub.astype(_F32)
    pos, seg_len = _segment_pos(u.shape, seq, n_ctx)
    left = (CONV_W - 1) // 2
    acc = cb_ref[...] + u * cw_ref[left:left + 1, :]
    for jj in range(CONV_W):
        if jj != left:
            acc = acc + _shift_rows(u, jj - left, pos, seg_len) * cw_ref[jj:jj + 1, :]
    xc = _silu(acc)
    xcb = xc.astype(_BF16)
    q = jnp.dot(xcb, wq_ref[...], preferred_element_type=_F32)
    k = jnp.dot(xcb, wk_ref[...], preferred_element_type=_F32)
    v = jnp.dot(ub, wv_ref[...], preferred_element_type=_F32)
    qb, kb, vb = q.astype(_BF16), k.astype(_BF16), v.astype(_BF16)
    xc_ref[...] = xcb
    q_ref[...] = qb
    k_ref[...] = (k * k_scale).astype(_BF16)
    v_ref[...] = vb
    part = jnp.dot(jnp.concatenate([qb, kb, vb], axis=1), wg_ref[...], preferred_element_type=_F32)

    @pl.when(j == 0)
    def _():
        g_ref[...] = part + gb_ref[...]

    @pl.when(j > 0)
    def _():
        g_ref[...] += part

    @pl.when(j == pl.num_programs(1) - 1)
    def _():
        g = g_ref[...]
        col = lax.broadcasted_iota(jnp.int32, g.shape, 1)
        log_sig = jnp.minimum(g, 0.0) - jnp.log1p(jnp.exp(-jnp.abs(g)))
        g_ref[...] = jnp.where(col < n_gate, g, log_sig)


def _blockdiag_tiles(w, te):
    nb = w.shape[0]
    per = te // QKV_BLOCK
    wt = w.reshape(nb // per, per, QKV_BLOCK, QKV_BLOCK)
    eye = jnp.eye(per, dtype=w.dtype)
    dense = jnp.einsum('tgio,gh->tgiho', wt, eye)
    return dense.reshape(nb // per, te, te).astype(_BF16)


def _mlstm_pre(uz, conv_w, conv_b, wq, wk, wv, w_ig, b_ig, w_fg, b_fg, seq):
    bsz, t_all, e2 = uz.shape
    e = e2 // 2
    nh = w_ig.shape[-1]
    te = LANES
    nt = e // te
    ng = 4 * nh
    wg = jnp.concatenate([w_ig[0], w_ig[1], w_fg[0], w_fg[1]], axis=-1)
    wg = wg.reshape(3, nt, te, ng).transpose(1, 0, 2, 3).reshape(nt, 3 * te, ng).astype(_BF16)
    gb = jnp.concatenate([b_ig[0], b_ig[1], b_fg[0], b_fg[1]], axis=-1).reshape(1, ng)
    tok = pl.BlockSpec((None, t_all, te), lambda b, j: (b, 0, j))
    bd = pl.BlockSpec((None, te, te), lambda b, j: (j, 0, 0))
    act = jax.ShapeDtypeStruct((bsz, t_all, e), _BF16)
    return pl.pallas_call(
        functools.partial(_mlstm_pre_kernel, seq=seq, n_ctx=t_all - seq,
                          k_scale=float((e // nh) ** -0.5), n_gate=2 * nh),
        grid=(bsz, nt),
        in_specs=[
            tok,
            pl.BlockSpec((CONV_W, te), lambda b, j: (0, j)),
            pl.BlockSpec((1, te), lambda b, j: (0, j)),
            bd, bd, bd,
            pl.BlockSpec((None, 3 * te, ng), lambda b, j: (j, 0, 0)),
            pl.BlockSpec((1, ng), lambda b, j: (0, 0)),
        ],
        out_specs=[tok, tok, tok, tok, pl.BlockSpec((None, t_all, ng), lambda b, j: (b, 0, 0))],
        out_shape=[act, act, act, act, jax.ShapeDtypeStruct((bsz, t_all, ng), _F32)],
        compiler_params=_cparams("parallel", "arbitrary"),
        name="mlstm_pre",
    )(uz, conv_w, conv_b.reshape(1, e), _blockdiag_tiles(wq, te), _blockdiag_tiles(wk, te),
      _blockdiag_tiles(wv, te), wg, gb)


def _scan_kernel(q_ref, k_ref, v_ref, li_ref, lf_ref, h_ref, c_sc, n_sc, m_sc):
    rev = pl.program_id(0)
    step = pl.program_id(3)

    @pl.when(step == 0)
    def _():
        c_sc[...] = jnp.zeros(c_sc.shape, _F32)
        n_sc[...] = jnp.zeros(n_sc.shape, _F32)
        m_sc[...] = jnp.zeros(m_sc.shape, _F32)

    ln = q_ref.shape[0]
    dh = c_sc.shape[-1]
    ti = lax.broadcasted_iota(jnp.int32, (ln, ln), 0)
    si = lax.broadcasted_iota(jnp.int32, (ln, ln), 1)
    sgn = 1 - 2 * rev
    incl = (si - ti) * sgn <= 0
    incl_t = (ti - si) * sgn <= 0
    eye = si == ti
    for hh in range(c_sc.shape[0]):
        cols = slice(hh * dh, (hh + 1) * dh)
        h_ref[:, cols] = _scan_chunk(q_ref[:, cols], k_ref[:, cols], v_ref[:, cols], li_ref[hh], lf_ref[hh],
                                     incl, incl_t, eye, c_sc.at[hh], n_sc.at[hh], m_sc.at[hh]).astype(h_ref.dtype)


def _scan_chunk(q, k, v, li_r, lf_r, incl, incl_t, eye, c_sc, n_sc, m_sc):
    ln = q.shape[0]
    lf_b = jnp.broadcast_to(lf_r, (ln, ln))
    li_b = jnp.broadcast_to(li_r, (ln, ln))
    b_col = jnp.sum(jnp.where(incl, lf_b, 0.0), axis=1, keepdims=True)
    lf_col = jnp.sum(jnp.where(eye, lf_b, 0.0), axis=1, keepdims=True)
    li_col = jnp.sum(jnp.where(eye, li_b, 0.0), axis=1, keepdims=True)
    b_row = jnp.sum(jnp.where(incl_t, lf_col, 0.0), axis=0, keepdims=True)
    b_end = jnp.sum(lf_r, axis=1, keepdims=True)
    m_old = m_sc[...]
    g_row = b_end - b_row + li_r
    g_col = b_end - b_col + li_col
    m_new = jnp.maximum(b_end + m_old, jnp.max(g_row, axis=1, keepdims=True))
    wg_col = jnp.exp(g_col - m_new)
    decay = jnp.exp(b_end + m_old - m_new)

    logw = jnp.where(incl, b_col - b_row + li_r, -jnp.inf)
    inter = b_col + m_old
    m_t = jnp.maximum(jnp.max(logw, axis=1, keepdims=True), inter)
    s = lax.dot_general(q, k, (((1,), (1,)), ((), ())), preferred_element_type=_F32) * jnp.exp(logw - m_t)
    w_inter = jnp.exp(inter - m_t)
    c_old = c_sc[...]
    num = (jnp.dot(s.astype(_BF16), v, preferred_element_type=_F32)
           + w_inter * jnp.dot(q, c_old.astype(_BF16), preferred_element_type=_F32))
    den = (jnp.sum(s, axis=1, keepdims=True)
           + w_inter * jnp.sum(q.astype(_F32) * n_sc[...], axis=1, keepdims=True))
    h = num / jnp.maximum(jnp.abs(den), jnp.exp(-m_t))

    vw = (v.astype(_F32) * wg_col).astype(_BF16)
    c_sc[...] = decay * c_old + lax.dot_general(k, vw, (((0,), (0,)), ((), ())),
                                                preferred_element_type=_F32)
    n_sc[...] = decay * n_sc[...] + jnp.sum(k.astype(_F32) * wg_col, axis=0, keepdims=True)
    m_sc[...] = m_new
    return h


def _mlstm_scan(q, k, v, gates, seq, nh):
    bsz, t_all, e = q.shape
    dh = e // nh
    n_ctx = t_all - seq
    ln = _tile(math.gcd(seq, n_ctx), SCAN_CHUNK, 8)
    nc = t_all // ln
    ncl = seq // ln
    ncc = n_ctx // ln
    g = gates.reshape(bsz, nc, ln, 2, 2, nh).transpose(3, 4, 0, 5, 1, 2).reshape(2, 2, bsz, nh, nc, 1, ln)

    def chunk(d, s):
        fwd = jnp.where(s < ncc, ncl + s, s - ncc)
        bwd = jnp.where(s < ncc, ncl + ncc - 1 - s, ncl - 1 - (s - ncc))
        return jnp.where(d == 0, fwd, bwd)

    hb = SCAN_HEADS_PER_STEP if nh % SCAN_HEADS_PER_STEP == 0 else 1
    tok = pl.BlockSpec((None, ln, hb * dh), lambda d, b, hh, s: (b, chunk(d, s), hh))
    gate = pl.BlockSpec((None, None, hb, None, 1, ln), lambda d, b, hh, s: (d, b, hh, chunk(d, s), 0, 0))
    return pl.pallas_call(
        _scan_kernel,
        grid=(2, bsz, nh // hb, nc),
        in_specs=[tok, tok, tok, gate, gate],
        out_specs=pl.BlockSpec((None, None, ln, hb * dh), lambda d, b, hh, s: (d, b, chunk(d, s), hh)),
        out_shape=jax.ShapeDtypeStruct((2, bsz, t_all, e), _BF16),
        scratch_shapes=[pltpu.VMEM((hb, dh, dh), _F32), pltpu.VMEM((hb, 1, dh), _F32),
                        pltpu.VMEM((hb, 1, 1), _F32)],
        compiler_params=_cparams("parallel", "parallel", "parallel", "arbitrary"),
        name="mlstm_scan",
    )(q, k, v, g[0], g[1])


def _mlstm_finish_kernel(hf_ref, hr_ref, xc_ref, z_ref, nw_ref, sk_ref, o_ref):
    h = hf_ref[...].astype(_F32) + hr_ref[...].astype(_F32)
    mu = jnp.mean(h, axis=-1, keepdims=True)
    hc = h - mu
    var = jnp.mean(hc * hc, axis=-1, keepdims=True)
    hn = hc * lax.rsqrt(var + EPS) * nw_ref[...]
    y = (hn + sk_ref[...] * xc_ref[...].astype(_F32)) * _silu(z_ref[...].astype(_F32))
    o_ref[...] = y.astype(o_ref.dtype)


def _mlstm_finish(h2, xc, uz, hnorm_w, skip, nh):
    bsz, t_all, e = xc.shape
    dh = e // nh
    tb = _tile(t_all, 544, 16)
    hspec = lambda d: pl.BlockSpec((None, None, tb, dh), lambda b, i, hh: (d, b, i, hh))
    tok = pl.BlockSpec((None, tb, dh), lambda b, i, hh: (b, i, hh))
    vec = pl.BlockSpec((1, dh), lambda b, i, hh: (0, hh))
    return pl.pallas_call(
        _mlstm_finish_kernel,
        grid=(bsz, t_all // tb, nh),
        in_specs=[hspec(0), hspec(1), tok,
                  pl.BlockSpec((None, tb, dh), lambda b, i, hh: (b, i, nh + hh)), vec, vec],
        out_specs=tok,
        out_shape=jax.ShapeDtypeStruct((bsz, t_all, e), _BF16),
        compiler_params=_cparams("parallel", "parallel", "parallel"),
        name="mlstm_finish",
    )(h2, h2, xc, uz, hnorm_w.reshape(1, e), skip.reshape(1, e))


def _mlstm_branch(uz, conv_w, conv_b, wq, wk, wv, w_ig, b_ig, w_fg, b_fg, hnorm_w, skip, seq):
    nh = w_ig.shape[-1]
    xc, q, k, v, gates = _mlstm_pre(uz, conv_w, conv_b, wq, wk, wv, w_ig, b_ig, w_fg, b_fg, seq)
    h2 = _mlstm_scan(q, k, v, gates, seq, nh)
    return _mlstm_finish(h2, xc, uz, hnorm_w, skip, nh)


def _pool_kernel(u_ref, o_ref, *, seq, n_ctx, tiles_per_group):
    grp = pl.program_id(1) // tiles_per_group
    u = u_ref[...].astype(_F32)
    pos, seg_len = _segment_pos(u.shape, seq, n_ctx)
    for gi, w in enumerate(POOL_WINDOWS):
        @pl.when(grp == gi)
        def _(w=w):
            lo = w // 2
            hi = w - 1 - lo
            trail = u
            lead = u
            span = 1
            while span < lo:
                trail = trail + _shift_rows(trail, -span, pos, seg_len)
                lead = lead + _shift_rows(lead, span, pos, seg_len)
                span *= 2
            win = _shift_rows(trail, -1, pos, seg_len) + lead
            cnt = jnp.minimum(pos + hi + 1, seg_len) - jnp.maximum(pos - lo, 0)
            o_ref[...] = (win / cnt.astype(_F32) - u).astype(o_ref.dtype)


def _pool_group_linear(a, w_grp, uz, scale):
    bsz, t_all, e = a.shape
    ng, gw, _ = w_grp.shape
    tm = _tile(t_all, 1088, 16)
    tn = _tile(gw, 1024, LANES)
    npg = gw // tn

    def epilogue(acc, pids, z_ref, sc_ref):
        return acc * sc_ref[...] * _silu(z_ref[...].astype(_F32))

    extras = [uz, scale.reshape(1, e)]
    especs = [pl.BlockSpec((None, tm, tn), lambda b, g, j, i, k: (b, i, (e // tn) + g * npg + j)),
              pl.BlockSpec((1, tn), lambda b, g, j, i, k: (0, g * npg + j))]
    return _matmul(
        (bsz, ng, npg, t_all // tm, 1),
        a, pl.BlockSpec((None, tm, gw), lambda b, g, j, i, k: (b, i, g)),
        w_grp, pl.BlockSpec((None, gw, tn), lambda b, g, j, i, k: (g, 0, j)),
        jax.ShapeDtypeStruct((bsz, t_all, e), _BF16),
        pl.BlockSpec((None, tm, tn), lambda b, g, j, i, k: (b, i, g * npg + j)),
        extras=tuple(extras), extra_specs=tuple(especs), epilogue=epilogue, name="pool_group")


def _pool_branch(uz, w_grp, scale, seq):
    bsz, t_all, e2 = uz.shape
    e = e2 // 2
    te = _tile(e // N_GROUPS, 256, LANES)
    tok = pl.BlockSpec((None, t_all, te), lambda b, j: (b, 0, j))
    dlt = pl.pallas_call(
        functools.partial(_pool_kernel, seq=seq, n_ctx=t_all - seq, tiles_per_group=e // N_GROUPS // te),
        grid=(bsz, e // te),
        in_specs=[tok],
        out_specs=tok,
        out_shape=jax.ShapeDtypeStruct((bsz, t_all, e), _BF16),
        compiler_params=_cparams("parallel", "parallel"),
        name="pool",
    )(uz)
    return _pool_group_linear(dlt, w_grp, uz, scale)


def _cos_sin(n_out, n_in, period):
    ang = (np.outer(np.arange(n_out), np.arange(n_in)) % period) * (2.0 * np.pi / period)
    return np.cos(ang), np.sin(ang)


@functools.lru_cache(maxsize=None)
def _chan_table(gw):
    c, s = _cos_sin(gw, gw, gw)
    return np.concatenate([c, s], axis=0) * gw ** -0.5


@functools.lru_cache(maxsize=None)
def _time_tables(seq, n_ctx):
    n1 = max(f for f in range(1, math.isqrt(seq) + 1) if seq % f == 0)
    n2 = seq // n1
    g = SUBLANES
    assert n1 % g == 0 and n2 % g == 0
    eye = np.eye(g)
    c1, s1 = _cos_sin(n1, n1, n1)
    base = np.block([[c1, -s1], [-s1, -c1]]) * seq ** -0.5
    stage1 = np.einsum('ab,rkst->arkstb', eye, base.reshape(2, n1, 2, n1)).reshape(g * 2 * n1, 2 * n1 * g)
    twc, tws = _cos_sin(n2, n1, seq)
    c2, s2 = _cos_sin(n2, n2, n2)
    cs2 = np.concatenate([c2, s2], axis=1)
    stage2 = np.einsum('ab,krt->kartb', eye, cs2.reshape(n2, 2, n2)).reshape(n2 * g, 2 * n2 * g)
    cc, sc = _cos_sin(n_ctx, n_ctx, n_ctx)
    ctx_tab = np.concatenate([cc, -sc], axis=1) * n_ctx ** -0.5
    lanes = lambda a: np.repeat(a[:, :, None], LANES, axis=2)
    return n1, n2, stage1, lanes(twc), lanes(tws), stage2, ctx_tab


def _time_dft_kernel(y1_ref, y2_ref, z_ref, w1_ref, twc_ref, tws_ref, w2_ref, wc_ref, o_ref,
                     y1f, y2f, ar, ai, out, *, seq, n1, n2):
    g = SUBLANES
    tc = o_ref.shape[-1]
    wide = lambda tab: jnp.concatenate([tab] * (tc // LANES), axis=1)
    y1f[...] = y1_ref[:seq, :].astype(_F32).reshape(y1f.shape)
    y2f[...] = y2_ref[:seq, :].astype(_F32).reshape(y2f.shape)
    for th in range(n2 // g):
        rhs = jnp.concatenate([y1f[:, th].reshape(n1 * g, tc), y2f[:, th].reshape(n1 * g, tc)], axis=0)
        res = jnp.dot(w1_ref[...], rhs.astype(_BF16), preferred_element_type=_F32)
        for tl in range(g):
            t2 = th * g + tl
            a_r = res[tl * 2 * n1:tl * 2 * n1 + n1]
            a_i = res[tl * 2 * n1 + n1:(tl + 1) * 2 * n1]
            c, s = wide(twc_ref[t2]), wide(tws_ref[t2])
            ar[t2] = (a_r * c + a_i * s).reshape(ar.shape[1:])
            ai[t2] = (a_i * c - a_r * s).reshape(ai.shape[1:])
    for kh in range(n1 // g):
        rhs = jnp.concatenate([ar[:, kh].reshape(n2 * g, tc), ai[:, kh].reshape(n2 * g, tc)], axis=0)
        res = jnp.dot(w2_ref[...], rhs.astype(_BF16), preferred_element_type=_F32)
        out[:, kh] = res.reshape(n2, g, tc)
    gate = _silu(z_ref[:seq, :].astype(_F32))
    o_ref[:seq, :] = (out[...].reshape(seq, tc) * gate).astype(o_ref.dtype)
    rhs = jnp.concatenate([y1_ref[seq:, :], y2_ref[seq:, :]], axis=0)
    res = jnp.dot(wc_ref[...], rhs, preferred_element_type=_F32)
    o_ref[seq:, :] = (res * _silu(z_ref[seq:, :].astype(_F32))).astype(o_ref.dtype)


def _fourier_branch(uz, w_grp, seq):
    bsz, t_all, e2 = uz.shape
    n_ctx = t_all - seq
    e = e2 // 2
    gw = e // N_GROUPS
    tab = jnp.asarray(_chan_table(gw), dtype=_BF16)
    tw = _tile(gw, 1024, LANES)
    nrow = gw // tw
    wcomb = _matmul(
        (N_GROUPS, gw // tw, 2 * nrow, 1),
        tab, pl.BlockSpec((tw, gw), lambda g, j, i, k: (i, 0)),
        w_grp, pl.BlockSpec((None, gw, tw), lambda g, j, i, k: (g, 0, j)),
        jax.ShapeDtypeStruct((N_GROUPS, gw, 2 * gw), _BF16),
        pl.BlockSpec((None, tw, tw), lambda g, j, i, k: (g, i % nrow, (i // nrow) * (gw // tw) + j)),
        name="fourier_weight")
    tm = _tile(t_all, 1088, 16)
    tn = _tile(gw, 1024, LANES)
    y = _matmul(
        (bsz, N_GROUPS, 2 * gw // tn, t_all // tm, 1),
        uz, pl.BlockSpec((None, tm, gw), lambda b, g, j, i, k: (b, i, g)),
        wcomb, pl.BlockSpec((None, gw, tn), lambda b, g, j, i, k: (g, 0, j)),
        jax.ShapeDtypeStruct((bsz, t_all, 2 * e), _BF16),
        pl.BlockSpec((None, tm, tn), lambda b, g, j, i, k: (b, i, g * (2 * gw // tn) + j)),
        name="dft_chan")
    n1, n2, stage1, twc, tws, stage2, ctx_tab = _time_tables(seq, n_ctx)
    tc = _tile(gw, 256, LANES)
    cpg = gw // tc

    def ycol(part):
        return lambda b, c: (b, 0, (2 * (c // cpg) + part) * cpg + c % cpg)

    tok = lambda imap: pl.BlockSpec((None, t_all, tc), imap)
    full = lambda a: pl.BlockSpec(a.shape, lambda b, c: (0,) * a.ndim, pipeline_mode=pl.Buffered(1))
    tabs = [jnp.asarray(stage1, dtype=_BF16), jnp.asarray(twc, dtype=_F32), jnp.asarray(tws, dtype=_F32),
            jnp.asarray(stage2, dtype=_BF16), jnp.asarray(ctx_tab, dtype=_BF16)]
    by_t1 = pltpu.VMEM((n1, n2 // SUBLANES, SUBLANES, tc), _F32)
    by_t2 = pltpu.VMEM((n2, n1 // SUBLANES, SUBLANES, tc), _F32)
    return pl.pallas_call(
        functools.partial(_time_dft_kernel, seq=seq, n1=n1, n2=n2),
        grid=(bsz, e // tc),
        in_specs=[tok(ycol(0)), tok(ycol(1)), tok(lambda b, c: (b, 0, e // tc + c))] + [full(t) for t in tabs],
        out_specs=tok(lambda b, c: (b, 0, c)),
        out_shape=jax.ShapeDtypeStruct((bsz, t_all, e), _BF16),
        scratch_shapes=[by_t1, by_t1, by_t2, by_t2, by_t2],
        compiler_params=_cparams("parallel", "parallel"),
        name="dft_time",
    )(y, y, uz, *tabs)


def kernel(x, c, ctx, c_ctx, ada_w, ada_b, norm_g, final_g, a_w_in, a_conv_w, a_conv_b, a_wq, a_wk, a_wv,
           a_w_ig, a_b_ig, a_w_fg, a_b_fg, a_hnorm_w, a_skip, a_w_out, b_w_in, b_w_grp, b_scale, b_w_out,
           c_w_in, c_w_grp, c_w_out):
    bsz, seq, d = x.shape
    depth = ada_w.shape[0]
    n_rows = 8
    assert bsz + 1 <= n_rows
    rows = jnp.concatenate([c, c_ctx[None], jnp.zeros((n_rows - bsz - 1, d), c.dtype)], axis=0)
    mods = _ada(rows, ada_w, ada_b).reshape(depth, n_rows, 1, 3 * d)
    xs = _embed(x, ctx)
    for i in range(depth):
        kind = i % N_MIXERS
        j = i // N_MIXERS
        h = _prenorm(xs, norm_g[i], mods[i], seq)
        w_in, w_out = ((a_w_in, a_w_out), (b_w_in, b_w_out), (c_w_in, c_w_out))[kind]
        uz = _in_proj(h, w_in, j)
        if kind == 0:
            a = _mlstm_branch(uz, a_conv_w[j], a_conv_b[j], a_wq[j], a_wk[j], a_wv[j],
                              a_w_ig[j], a_b_ig[j], a_w_fg[j], a_b_fg[j], a_hnorm_w[j], a_skip[j], seq)
        elif kind == 1:
            a = _pool_branch(uz, b_w_grp[j], b_scale[j], seq)
        else:
            a = _fourier_branch(uz, c_w_grp[j], seq)
        xs = _out_proj(a, w_out, j, xs, mods[i], seq)
    return _final_norm(xs, final_g, seq)
```

```python
import functools
import math

import jax
import jax.numpy as jnp
import numpy as np
from jax import lax
from jax.experimental import pallas as pl
from jax.experimental.pallas import tpu as pltpu

GRID_W = 64
N_MIXERS = 3
MLSTM_HEADS = 8
QKV_BLOCK = 4
CONV_W = 4
POOL_WINDOWS = (2, 4, 8, 16)
N_GROUPS = 4
POS_BASE = 10000.0
EPS = 1e-6

LANES = 128
SUBLANES = 8
VMEM_LIMIT_BYTES = 56 * 1024 * 1024
SCAN_CHUNK = 256
SCAN_HEADS_PER_STEP = 2

_F32 = jnp.float32
_BF16 = jnp.bfloat16


def _cparams(*sem):
    return pltpu.CompilerParams(dimension_semantics=sem, vmem_limit_bytes=VMEM_LIMIT_BYTES)


def _tile(n, target, align):
    best = None
    for t in range(align, min(n, target) + 1, align):
        if n % t == 0:
            best = t
    return n if best is None else best


def _silu(v):
    return v * (1.0 / (1.0 + jnp.exp(-v)))


def _matmul(grid, a, a_spec, b, b_spec, out_shape, out_spec, extras=(), extra_specs=(), epilogue=None,
            name="matmul"):
    n_extra = len(extras)
    if epilogue is None:
        epilogue = lambda acc, pids: acc

    def body(*refs):
        a_ref, b_ref = refs[0], refs[1]
        ex = refs[2:2 + n_extra]
        o_ref = refs[2 + n_extra]
        pids = tuple(pl.program_id(ax) for ax in range(len(grid)))
        acc = jnp.dot(a_ref[...].astype(_BF16), b_ref[...].astype(_BF16), preferred_element_type=_F32)
        o_ref[...] = epilogue(acc, pids, *ex).astype(o_ref.dtype)

    return pl.pallas_call(
        body,
        grid=grid,
        in_specs=[a_spec, b_spec, *extra_specs],
        out_specs=out_spec,
        out_shape=out_shape,
        compiler_params=_cparams(*(("parallel",) * len(grid))),
        name=name,
    )(a, b, *extras)


def _proj(a, w, layer, tm, tn, out_dtype, name, extras=(), extra_blocks=(), epilogue=None):
    bsz, t_all, k = a.shape
    n = w.shape[2]
    nt, mt = n // tn, t_all // tm
    pieces = bsz * mt
    tkp = k // pieces
    assert tkp * pieces == k and tkp % 16 == 0
    n_extra = len(extras)
    if epilogue is None:
        epilogue = lambda acc, pids: acc

    def body(*refs):
        a_ref, w_ref = refs[0], refs[1]
        ex = refs[2:2 + n_extra]
        o_ref, w_bf = refs[2 + n_extra], refs[3 + n_extra]
        pids = tuple(pl.program_id(ax) for ax in range(3))
        jj = pids[0]
        piece = pids[1] * mt + pids[2]

        def stage():
            rows = pl.ds(pl.multiple_of(piece * tkp, tkp), tkp)
            w_bf[jj % 2, rows, :] = w_ref[...].astype(_BF16)

        @pl.when(jj == 0)
        def _():
            stage()

        @pl.when(jj > 0)
        def _():
            stage()
            acc = jnp.dot(a_ref[...], w_bf[(jj + 1) % 2], preferred_element_type=_F32)
            o_ref[...] = epilogue(acc, pids, *ex).astype(o_ref.dtype)

    def a_map(jj, b, i):
        return (jnp.where(jj > 0, b, 0), jnp.where(jj > 0, i, 0), 0)

    def o_map(jj, b, i):
        return (jnp.where(jj > 0, b, 0), jnp.where(jj > 0, i, 0), jnp.maximum(jj - 1, 0))

    def w_map(jj, b, i):
        last = jj == nt
        return (layer, jnp.where(last, pieces - 1, b * mt + i), jnp.minimum(jj, nt - 1))

    return pl.pallas_call(
        body,
        grid=(nt + 1, bsz, mt),
        in_specs=[pl.BlockSpec((None, tm, k), a_map),
                  pl.BlockSpec((None, tkp, tn), w_map),
                  *[spec(o_map) for spec in extra_blocks]],
        out_specs=pl.BlockSpec((None, tm, tn), o_map),
        out_shape=jax.ShapeDtypeStruct((bsz, t_all, n), out_dtype),
        scratch_shapes=[pltpu.VMEM((2, k, tn), _BF16)],
        compiler_params=_cparams("arbitrary", "arbitrary", "arbitrary"),
        name=name,
    )(a, w, *extras)


def _ada_kernel(s_ref, w_ref, b_ref, o_ref):
    s = _silu(s_ref[...]).astype(_BF16)
    acc = jnp.dot(s, w_ref[...].astype(_BF16), preferred_element_type=_F32)
    o_ref[...] = acc + b_ref[...]


def _ada(rows, ada_w, ada_b):
    depth, d, n3 = ada_w.shape
    nr = rows.shape[0]
    tn = _tile(n3, 512, LANES)
    return pl.pallas_call(
        _ada_kernel,
        grid=(depth, n3 // tn),
        in_specs=[
            pl.BlockSpec((nr, d), lambda i, n: (0, 0)),
            pl.BlockSpec((None, d, tn), lambda i, n: (i, 0, n)),
            pl.BlockSpec((None, 1, tn), lambda i, n: (i, 0, n)),
        ],
        out_specs=pl.BlockSpec((None, nr, tn), lambda i, n: (i, 0, n)),
        out_shape=jax.ShapeDtypeStruct((depth, nr, n3), _F32),
        compiler_params=_cparams("parallel", "parallel"),
        name="ada",
    )(rows, ada_w, ada_b.reshape(depth, 1, n3))


def _embed_kernel(x_ref, c_ref, re_ref, ce_ref, o_ref, *, n_rows, half):
    r = pl.program_id(1)

    @pl.when(r < n_rows)
    def _():
        o_ref[:, :half] = x_ref[:, :half] + re_ref[...]
        o_ref[:, half:] = x_ref[:, half:] + ce_ref[...]

    @pl.when(r >= n_rows)
    def _():
        o_ref[...] = c_ref[...]


def _embed(x, ctx):
    bsz, seq, d = x.shape
    n_ctx = ctx.shape[1]
    n_rows = seq // GRID_W
    n_cblk = n_ctx // GRID_W
    half = d // 2
    quarter = d // 4
    omega = 1.0 / (POS_BASE ** (jnp.arange(quarter, dtype=_F32) / quarter))

    def axis_emb(p):
        ang = p[:, None] * omega[None, :]
        return jnp.concatenate([jnp.sin(ang), jnp.cos(ang)], axis=-1)

    row_emb = axis_emb(jnp.arange(n_rows, dtype=_F32)).reshape(n_rows, 1, half)
    col_emb = axis_emb(jnp.arange(GRID_W, dtype=_F32))
    return pl.pallas_call(
        functools.partial(_embed_kernel, n_rows=n_rows, half=half),
        grid=(bsz, n_rows + n_cblk),
        in_specs=[
            pl.BlockSpec((None, GRID_W, d), lambda b, r: (b, jnp.minimum(r, n_rows - 1), 0)),
            pl.BlockSpec((None, GRID_W, d), lambda b, r: (b, jnp.maximum(r - n_rows, 0), 0)),
            pl.BlockSpec((None, 1, half), lambda b, r: (jnp.minimum(r, n_rows - 1), 0, 0)),
            pl.BlockSpec((GRID_W, half), lambda b, r: (0, 0)),
        ],
        out_specs=pl.BlockSpec((None, GRID_W, d), lambda b, r: (b, r, 0)),
        out_shape=jax.ShapeDtypeStruct((bsz, seq + n_ctx, d), x.dtype),
        compiler_params=_cparams("parallel", "parallel"),
        name="embed",
    )(x, ctx, row_emb, col_emb)


def _prenorm_kernel(x_ref, g_ref, sh_ref, sc_ref, o_ref):
    x = x_ref[...]
    y = x * lax.rsqrt(jnp.mean(x * x, axis=-1, keepdims=True) + EPS)
    o_ref[...] = (y * g_ref[...] * (1.0 + sc_ref[...]) + sh_ref[...]).astype(o_ref.dtype)


def _prenorm(xs, g, mods, seq):
    bsz, t_all, d = xs.shape
    tb = _tile(math.gcd(seq, t_all - seq), 256, 8)
    n_lat = seq // tb

    def mrow(b, j):
        return jnp.where(j < n_lat, b, bsz)

    return pl.pallas_call(
        _prenorm_kernel,
        grid=(bsz, t_all // tb),
        in_specs=[
            pl.BlockSpec((None, tb, d), lambda b, j: (b, j, 0)),
            pl.BlockSpec((1, d), lambda b, j: (0, 0)),
            pl.BlockSpec((None, 1, d), lambda b, j: (mrow(b, j), 0, 0)),
            pl.BlockSpec((None, 1, d), lambda b, j: (mrow(b, j), 0, 1)),
        ],
        out_specs=pl.BlockSpec((None, tb, d), lambda b, j: (b, j, 0)),
        out_shape=jax.ShapeDtypeStruct((bsz, t_all, d), _BF16),
        compiler_params=_cparams("parallel", "parallel"),
        name="prenorm",
    )(xs, g.reshape(1, d), mods, mods)


def _final_norm_kernel(x_ref, g_ref, o_ref):
    x = x_ref[...]
    y = x * lax.rsqrt(jnp.mean(x * x, axis=-1, keepdims=True) + EPS)
    o_ref[...] = y * g_ref[...]


def _final_norm(xs, g, seq):
    bsz, t_all, d = xs.shape
    tb = _tile(math.gcd(seq, t_all - seq), 256, 8)
    return pl.pallas_call(
        _final_norm_kernel,
        grid=(bsz, seq // tb),
        in_specs=[
            pl.BlockSpec((None, tb, d), lambda b, j: (b, j, 0)),
            pl.BlockSpec((1, d), lambda b, j: (0, 0)),
        ],
        out_specs=pl.BlockSpec((None, tb, d), lambda b, j: (b, j, 0)),
        out_shape=jax.ShapeDtypeStruct((bsz, seq, d), xs.dtype),
        compiler_params=_cparams("parallel", "parallel"),
        name="final_norm",
    )(xs, g.reshape(1, d))


def _in_proj(h, w, layer):
    t_all, n = h.shape[1], w.shape[2]
    return _proj(h, w, layer, _tile(t_all, 1088, 16), _tile(n, 1024, LANES), _BF16, "in_proj")


def _out_proj(a, w, layer, xs, mods, seq):
    bsz, t_all, _ = a.shape
    d = w.shape[2]
    tm = _tile(t_all, 544, 16)
    tn = _tile(d, 512, LANES)
    g_off = 2 * (d // tn)

    def epilogue(acc, pids, res_ref, gl_ref, gc_ref):
        row = pids[2] * tm + lax.broadcasted_iota(jnp.int32, acc.shape, 0)
        gate = jnp.where(row < seq, gl_ref[...], gc_ref[...])
        return res_ref[...] + gate * acc

    def gate_block(row_of):
        return lambda om: pl.BlockSpec((None, 1, tn), lambda jj, b, i: (row_of(b), 0, g_off + om(jj, b, i)[2]))

    return _proj(a, w, layer, tm, tn, xs.dtype, "out_proj", extras=(xs, mods, mods),
                 extra_blocks=(lambda om: pl.BlockSpec((None, tm, tn), om),
                               gate_block(lambda b: b), gate_block(lambda b: bsz)),
                 epilogue=epilogue)


EDGE_ROWS = 16
EDGE_HALO = 16


def _segment_pos(shape, seq, n_ctx, row0):
    t = row0 + lax.broadcasted_iota(jnp.int32, shape, 0)
    is_lat = t < seq
    return jnp.where(is_lat, t, t - seq), jnp.where(is_lat, seq, n_ctx)


def _shift_rows(u, k, bounds=None):
    rolled = pltpu.roll(u, (-k) % u.shape[0], axis=0)
    if bounds is None:
        return rolled
    pos, seg_len = bounds
    valid = (pos >= -k) if k < 0 else (pos < seg_len - k)
    return jnp.where(valid, rolled, 0.0)


def _edge_windows(seq, t_all):
    p, h = EDGE_ROWS, EDGE_HALO
    assert seq >= p + h and t_all - seq >= 2 * (p + h)
    return ((0, p + h, 0, p), (seq - p - h, 2 * (p + h), h, 2 * p), (t_all - p - h, p + h, h, p))


def _redo_edges(fn, u_ref, dst_ref, seq, t_all):
    for row0, rows, off, n in _edge_windows(seq, t_all):
        uw = u_ref[row0:row0 + rows, :].astype(_F32)
        res = fn(uw, _segment_pos(uw.shape, seq, t_all - seq, row0))
        dst_ref[row0 + off:row0 + off + n, :] = res[off:off + n].astype(dst_ref.dtype)


def _mlstm_pre_kernel(u_ref, cw_ref, cb_ref, wq_ref, wk_ref, wv_ref, wg_ref, gb_ref,
                      xc_ref, q_ref, k_ref, v_ref, g_ref, acc_sc, *, seq, n_ctx, k_scale, n_gate):
    j = pl.program_id(1)
    ub = u_ref[...]
    left = (CONV_W - 1) // 2

    def conv(u, bounds):
        acc = cb_ref[...] + u * cw_ref[left:left + 1, :]
        for jj in range(CONV_W):
            if jj != left:
                acc = acc + _shift_rows(u, jj - left, bounds) * cw_ref[jj:jj + 1, :]
        return acc

    acc_sc[...] = conv(ub.astype(_F32), None)
    _redo_edges(conv, u_ref, acc_sc, seq, seq + n_ctx)
    xc = _silu(acc_sc[...])
    xcb = xc.astype(_BF16)
    q = jnp.dot(xcb, wq_ref[...], preferred_element_type=_F32)
    k = jnp.dot(xcb, wk_ref[...], preferred_element_type=_F32)
    v = jnp.dot(ub, wv_ref[...], preferred_element_type=_F32)
    qb, kb, vb = q.astype(_BF16), k.astype(_BF16), v.astype(_BF16)
    xc_ref[...] = xcb
    q_ref[...] = qb
    k_ref[...] = (k * k_scale).astype(_BF16)
    v_ref[...] = vb
    part = jnp.dot(jnp.concatenate([qb, kb, vb], axis=1), wg_ref[...], preferred_element_type=_F32)

    @pl.when(j == 0)
    def _():
        g_ref[...] = part + gb_ref[...]

    @pl.when(j > 0)
    def _():
        g_ref[...] += part

    @pl.when(j == pl.num_programs(1) - 1)
    def _():
        g = g_ref[...]
        col = lax.broadcasted_iota(jnp.int32, g.shape, 1)
        log_sig = jnp.minimum(g, 0.0) - jnp.log1p(jnp.exp(-jnp.abs(g)))
        g_ref[...] = jnp.where(col < n_gate, g, log_sig)


def _blockdiag_tiles(w, te):
    nb = w.shape[0]
    per = te // QKV_BLOCK
    wt = w.reshape(nb // per, per, QKV_BLOCK, QKV_BLOCK)
    eye = jnp.eye(per, dtype=w.dtype)
    dense = jnp.einsum('tgio,gh->tgiho', wt, eye)
    return dense.reshape(nb // per, te, te).astype(_BF16)


def _mlstm_pre(uz, conv_w, conv_b, wq, wk, wv, w_ig, b_ig, w_fg, b_fg, seq):
    bsz, t_all, e2 = uz.shape
    e = e2 // 2
    nh = w_ig.shape[-1]
    te = _tile(e, 256, LANES)
    nt = e // te
    ng = 4 * nh
    wg = jnp.concatenate([w_ig[0], w_ig[1], w_fg[0], w_fg[1]], axis=-1)
    wg = wg.reshape(3, nt, te, ng).transpose(1, 0, 2, 3).reshape(nt, 3 * te, ng).astype(_BF16)
    gb = jnp.concatenate([b_ig[0], b_ig[1], b_fg[0], b_fg[1]], axis=-1).reshape(1, ng)
    tok = pl.BlockSpec((None, t_all, te), lambda b, j: (b, 0, j))
    bd = pl.BlockSpec((None, te, te), lambda b, j: (j, 0, 0))
    act = jax.ShapeDtypeStruct((bsz, t_all, e), _BF16)
    return pl.pallas_call(
        functools.partial(_mlstm_pre_kernel, seq=seq, n_ctx=t_all - seq,
                          k_scale=float((e // nh) ** -0.5), n_gate=2 * nh),
        grid=(bsz, nt),
        in_specs=[
            tok,
            pl.BlockSpec((CONV_W, te), lambda b, j: (0, j)),
            pl.BlockSpec((1, te), lambda b, j: (0, j)),
            bd, bd, bd,
            pl.BlockSpec((None, 3 * te, ng), lambda b, j: (j, 0, 0)),
            pl.BlockSpec((1, ng), lambda b, j: (0, 0)),
        ],
        out_specs=[tok, tok, tok, tok, pl.BlockSpec((None, t_all, ng), lambda b, j: (b, 0, 0))],
        out_shape=[act, act, act, act, jax.ShapeDtypeStruct((bsz, t_all, ng), _F32)],
        scratch_shapes=[pltpu.VMEM((t_all, te), _F32)],
        compiler_params=_cparams("parallel", "arbitrary"),
        name="mlstm_pre",
    )(uz, conv_w, conv_b.reshape(1, e), _blockdiag_tiles(wq, te), _blockdiag_tiles(wk, te),
      _blockdiag_tiles(wv, te), wg, gb)


def _scan_kernel(q_ref, k_ref, v_ref, li_ref, lf_ref, *rest, rev):
    if rev:
        hf_ref, xc_ref, z_ref, nw_ref, sk_ref, o_ref, c_sc, n_sc, m_sc = rest
    else:
        o_ref, c_sc, n_sc, m_sc = rest

    @pl.when(pl.program_id(2) == 0)
    def _():
        c_sc[...] = jnp.zeros(c_sc.shape, _F32)
        n_sc[...] = jnp.zeros(n_sc.shape, _F32)
        m_sc[...] = jnp.zeros(m_sc.shape, _F32)

    ln = q_ref.shape[0]
    dh = c_sc.shape[-1]
    ti = lax.broadcasted_iota(jnp.int32, (ln, ln), 0)
    si = lax.broadcasted_iota(jnp.int32, (ln, ln), 1)
    incl = si >= ti if rev else si <= ti
    incl_t = ti >= si if rev else ti <= si
    eye = si == ti
    for hh in range(c_sc.shape[0]):
        cols = slice(hh * dh, (hh + 1) * dh)
        h = _scan_chunk(q_ref[:, cols], k_ref[:, cols], v_ref[:, cols], li_ref[hh], lf_ref[hh],
                        incl, incl_t, eye, c_sc.at[hh], n_sc.at[hh], m_sc.at[hh])
        if rev:
            h = h + hf_ref[:, cols].astype(_F32)
            mu = jnp.mean(h, axis=-1, keepdims=True)
            hc = h - mu
            var = jnp.mean(hc * hc, axis=-1, keepdims=True)
            hn = hc * lax.rsqrt(var + EPS) * nw_ref[:, cols]
            h = (hn + sk_ref[:, cols] * xc_ref[:, cols].astype(_F32)) * _silu(z_ref[:, cols].astype(_F32))
        o_ref[:, cols] = h.astype(o_ref.dtype)


def _scan_chunk(q, k, v, li_r, lf_r, incl, incl_t, eye, c_sc, n_sc, m_sc):
    ln = q.shape[0]
    lf_b = jnp.broadcast_to(lf_r, (ln, ln))
    li_b = jnp.broadcast_to(li_r, (ln, ln))
    b_col = jnp.sum(jnp.where(incl, lf_b, 0.0), axis=1, keepdims=True)
    lf_col = jnp.sum(jnp.where(eye, lf_b, 0.0), axis=1, keepdims=True)
    li_col = jnp.sum(jnp.where(eye, li_b, 0.0), axis=1, keepdims=True)
    b_row = jnp.sum(jnp.where(incl_t, lf_col, 0.0), axis=0, keepdims=True)
    b_end = jnp.sum(lf_r, axis=1, keepdims=True)
    m_old = m_sc[...]
    g_row = b_end - b_row + li_r
    g_col = b_end - b_col + li_col
    m_new = jnp.maximum(b_end + m_old, jnp.max(g_row, axis=1, keepdims=True))
    wg_col = jnp.exp(g_col - m_new)
    decay = jnp.exp(b_end + m_old - m_new)

    logw = jnp.where(incl, b_col - b_row + li_r, -jnp.inf)
    inter = b_col + m_old
    m_t = jnp.maximum(jnp.max(logw, axis=1, keepdims=True), inter)
    s = lax.dot_general(q, k, (((1,), (1,)), ((), ())), preferred_element_type=_F32) * jnp.exp(logw - m_t)
    w_inter = jnp.exp(inter - m_t)
    c_old = c_sc[...]
    num = (jnp.dot(s.astype(_BF16), v, preferred_element_type=_F32)
           + w_inter * jnp.dot(q, c_old.astype(_BF16), preferred_element_type=_F32))
    den = (jnp.sum(s, axis=1, keepdims=True)
           + w_inter * jnp.sum(q.astype(_F32) * n_sc[...], axis=1, keepdims=True))
    h = num / jnp.maximum(jnp.abs(den), jnp.exp(-m_t))

    vw = (v.astype(_F32) * wg_col).astype(_BF16)
    c_sc[...] = decay * c_old + lax.dot_general(k, vw, (((0,), (0,)), ((), ())),
                                                preferred_element_type=_F32)
    n_sc[...] = decay * n_sc[...] + jnp.sum(k.astype(_F32) * wg_col, axis=0, keepdims=True)
    m_sc[...] = m_new
    return h


def _mlstm_scan(q, k, v, gates, uz, xc, hnorm_w, skip, seq, nh):
    bsz, t_all, e = q.shape
    dh = e // nh
    n_ctx = t_all - seq
    ln = _tile(math.gcd(seq, n_ctx), SCAN_CHUNK, 8)
    nc = t_all // ln
    ncl = seq // ln
    ncc = n_ctx // ln
    g = gates.reshape(bsz, nc, ln, 2, 2, nh).transpose(3, 4, 0, 5, 1, 2).reshape(2, 2, bsz, nh, nc, 1, ln)
    hb = SCAN_HEADS_PER_STEP if nh % SCAN_HEADS_PER_STEP == 0 else 1
    act = jax.ShapeDtypeStruct((bsz, t_all, e), _BF16)
    scratch = [pltpu.VMEM((hb, dh, dh), _F32), pltpu.VMEM((hb, 1, dh), _F32), pltpu.VMEM((hb, 1, 1), _F32)]

    def run(rev, extra_in, extra_specs):
        def chunk(s):
            if rev:
                return jnp.where(s < ncc, ncl + ncc - 1 - s, ncl - 1 - (s - ncc))
            return jnp.where(s < ncc, ncl + s, s - ncc)

        tok = lambda col0: pl.BlockSpec((None, ln, hb * dh), lambda b, hh, s: (b, chunk(s), col0 + hh))
        gate = pl.BlockSpec((None, hb, None, 1, ln), lambda b, hh, s: (b, hh, chunk(s), 0, 0))
        return pl.pallas_call(
            functools.partial(_scan_kernel, rev=rev),
            grid=(bsz, nh // hb, nc),
            in_specs=[tok(0), tok(0), tok(0), gate, gate, *extra_specs(tok)],
            out_specs=tok(0),
            out_shape=act,
            scratch_shapes=scratch,
            compiler_params=_cparams("parallel", "parallel", "arbitrary"),
            name="mlstm_scan_rev" if rev else "mlstm_scan_fwd",
        )(q, k, v, g[0, int(rev)], g[1, int(rev)], *extra_in)

    h_fwd = run(False, (), lambda tok: ())
    vec = pl.BlockSpec((1, hb * dh), lambda b, hh, s: (0, hh))
    return run(True, (h_fwd, xc, uz, hnorm_w.reshape(1, e), skip.reshape(1, e)),
               lambda tok: (tok(0), tok(0), tok(nh // hb), vec, vec))


def _mlstm_branch(uz, conv_w, conv_b, wq, wk, wv, w_ig, b_ig, w_fg, b_fg, hnorm_w, skip, seq):
    nh = w_ig.shape[-1]
    xc, q, k, v, gates = _mlstm_pre(uz, conv_w, conv_b, wq, wk, wv, w_ig, b_ig, w_fg, b_fg, seq)
    return _mlstm_scan(q, k, v, gates, uz, xc, hnorm_w, skip, seq, nh)


def _pool_kernel(u_ref, o_ref, *, seq, n_ctx, tiles_per_group):
    grp = pl.program_id(1) // tiles_per_group
    for gi, w in enumerate(POOL_WINDOWS):
        lo = w // 2
        hi = w - 1 - lo
        assert hi == lo - 1 and lo <= EDGE_ROWS

        def delta(u, bounds, w=w, lo=lo, hi=hi):
            trail = u
            lead = u
            span = 1
            while span < lo:
                trail = trail + _shift_rows(trail, -span, bounds)
                lead = lead + _shift_rows(lead, span, bounds)
                span *= 2
            win = _shift_rows(trail, -1, bounds) + lead
            if bounds is None:
                return win * (1.0 / w) - u
            pos, seg_len = bounds
            cnt = jnp.minimum(pos + hi + 1, seg_len) - jnp.maximum(pos - lo, 0)
            return win / cnt.astype(_F32) - u

        @pl.when(grp == gi)
        def _(delta=delta):
            o_ref[...] = delta(u_ref[...].astype(_F32), None).astype(o_ref.dtype)
            _redo_edges(delta, u_ref, o_ref, seq, seq + n_ctx)


def _pool_group_linear(a, w_grp, uz, scale):
    bsz, t_all, e = a.shape
    ng, gw, _ = w_grp.shape
    tm = _tile(t_all, 1088, 16)
    tn = _tile(gw, 1024, LANES)
    npg = gw // tn

    def epilogue(acc, pids, z_ref, sc_ref):
        return acc * sc_ref[...] * _silu(z_ref[...].astype(_F32))

    extras = [uz, scale.reshape(1, e)]
    especs = [pl.BlockSpec((None, tm, tn), lambda b, g, j, i, k: (b, i, (e // tn) + g * npg + j)),
              pl.BlockSpec((1, tn), lambda b, g, j, i, k: (0, g * npg + j))]
    return _matmul(
        (bsz, ng, npg, t_all // tm, 1),
        a, pl.BlockSpec((None, tm, gw), lambda b, g, j, i, k: (b, i, g)),
        w_grp, pl.BlockSpec((None, gw, tn), lambda b, g, j, i, k: (g, 0, j)),
        jax.ShapeDtypeStruct((bsz, t_all, e), _BF16),
        pl.BlockSpec((None, tm, tn), lambda b, g, j, i, k: (b, i, g * npg + j)),
        extras=tuple(extras), extra_specs=tuple(especs), epilogue=epilogue, name="pool_group")


def _pool_branch(uz, w_grp, scale, seq):
    bsz, t_all, e2 = uz.shape
    e = e2 // 2
    te = _tile(e // N_GROUPS, 256, LANES)
    tok = pl.BlockSpec((None, t_all, te), lambda b, j: (b, 0, j))
    dlt = pl.pallas_call(
        functools.partial(_pool_kernel, seq=seq, n_ctx=t_all - seq, tiles_per_group=e // N_GROUPS // te),
        grid=(bsz, e // te),
        in_specs=[tok],
        out_specs=tok,
        out_shape=jax.ShapeDtypeStruct((bsz, t_all, e), _BF16),
        compiler_params=_cparams("parallel", "parallel"),
        name="pool",
    )(uz)
    return _pool_group_linear(dlt, w_grp, uz, scale)


def _cos_sin(n_out, n_in, period):
    ang = (np.outer(np.arange(n_out), np.arange(n_in)) % period) * (2.0 * np.pi / period)
    return np.cos(ang), np.sin(ang)


@functools.lru_cache(maxsize=None)
def _chan_table(gw):
    c, s = _cos_sin(gw, gw, gw)
    return np.concatenate([c, s], axis=0) * gw ** -0.5


@functools.lru_cache(maxsize=None)
def _time_tables(seq, n_ctx):
    n1 = max(f for f in range(1, math.isqrt(seq) + 1) if seq % f == 0)
    n2 = seq // n1
    g = SUBLANES
    assert n1 % g == 0 and n2 % g == 0
    eye = np.eye(g)
    c1, s1 = _cos_sin(n1, n1, n1)
    base = np.block([[c1, -s1], [-s1, -c1]]) * seq ** -0.5
    stage1 = np.einsum('ab,rkst->arkstb', eye, base.reshape(2, n1, 2, n1)).reshape(g * 2 * n1, 2 * n1 * g)
    twc, tws = _cos_sin(n2, n1, seq)
    c2, s2 = _cos_sin(n2, n2, n2)
    cs2 = np.concatenate([c2, s2], axis=1)
    stage2 = np.einsum('ab,krt->kartb', eye, cs2.reshape(n2, 2, n2)).reshape(n2 * g, 2 * n2 * g)
    cc, sc = _cos_sin(n_ctx, n_ctx, n_ctx)
    ctx_tab = np.concatenate([cc, -sc], axis=1) * n_ctx ** -0.5
    lanes = lambda a: np.repeat(a[:, :, None], LANES, axis=2)
    return n1, n2, stage1, lanes(twc), lanes(tws), stage2, ctx_tab


def _time_dft_kernel(y1_ref, y2_ref, z_ref, w1_ref, twc_ref, tws_ref, w2_ref, wc_ref, o_ref,
                     y1f, y2f, ar, ai, out, *, seq, n1, n2):
    g = SUBLANES
    tc = o_ref.shape[-1]
    wide = lambda tab: jnp.concatenate([tab] * (tc // LANES), axis=1)
    y1f[...] = y1_ref[:seq, :].astype(_F32).reshape(y1f.shape)
    y2f[...] = y2_ref[:seq, :].astype(_F32).reshape(y2f.shape)
    for th in range(n2 // g):
        rhs = jnp.concatenate([y1f[:, th].reshape(n1 * g, tc), y2f[:, th].reshape(n1 * g, tc)], axis=0)
        res = jnp.dot(w1_ref[...], rhs.astype(_BF16), preferred_element_type=_F32)
        for tl in range(g):
            t2 = th * g + tl
            a_r = res[tl * 2 * n1:tl * 2 * n1 + n1]
            a_i = res[tl * 2 * n1 + n1:(tl + 1) * 2 * n1]
            c, s = wide(twc_ref[t2]), wide(tws_ref[t2])
            ar[t2] = (a_r * c + a_i * s).reshape(ar.shape[1:])
            ai[t2] = (a_i * c - a_r * s).reshape(ai.shape[1:])
    for kh in range(n1 // g):
        rhs = jnp.concatenate([ar[:, kh].reshape(n2 * g, tc), ai[:, kh].reshape(n2 * g, tc)], axis=0)
        res = jnp.dot(w2_ref[...], rhs.astype(_BF16), preferred_element_type=_F32)
        out[:, kh] = res.reshape(n2, g, tc)
    gate = _silu(z_ref[:seq, :].astype(_F32))
    o_ref[:seq, :] = (out[...].reshape(seq, tc) * gate).astype(o_ref.dtype)
    rhs = jnp.concatenate([y1_ref[seq:, :], y2_ref[seq:, :]], axis=0)
    res = jnp.dot(wc_ref[...], rhs, preferred_element_type=_F32)
    o_ref[seq:, :] = (res * _silu(z_ref[seq:, :].astype(_F32))).astype(o_ref.dtype)


def _fourier_branch(uz, w_grp, seq):
    bsz, t_all, e2 = uz.shape
    n_ctx = t_all - seq
    e = e2 // 2
    gw = e // N_GROUPS
    tab = jnp.asarray(_chan_table(gw), dtype=_BF16)
    tw = _tile(gw, 1024, LANES)
    nrow = gw // tw
    wcomb = _matmul(
        (N_GROUPS, gw // tw, 2 * nrow, 1),
        tab, pl.BlockSpec((tw, gw), lambda g, j, i, k: (i, 0)),
        w_grp, pl.BlockSpec((None, gw, tw), lambda g, j, i, k: (g, 0, j)),
        jax.ShapeDtypeStruct((N_GROUPS, gw, 2 * gw), _BF16),
        pl.BlockSpec((None, tw, tw), lambda g, j, i, k: (g, i % nrow, (i // nrow) * (gw // tw) + j)),
        name="fourier_weight")
    tm = _tile(t_all, 1088, 16)
    tn = _tile(gw, 1024, LANES)
    y = _matmul(
        (bsz, N_GROUPS, 2 * gw // tn, t_all // tm, 1),
        uz, pl.BlockSpec((None, tm, gw), lambda b, g, j, i, k: (b, i, g)),
        wcomb, pl.BlockSpec((None, gw, tn), lambda b, g, j, i, k: (g, 0, j)),
        jax.ShapeDtypeStruct((bsz, t_all, 2 * e), _BF16),
        pl.BlockSpec((None, tm, tn), lambda b, g, j, i, k: (b, i, g * (2 * gw // tn) + j)),
        name="dft_chan")
    n1, n2, stage1, twc, tws, stage2, ctx_tab = _time_tables(seq, n_ctx)
    tc = _tile(gw, 256, LANES)
    cpg = gw // tc

    def ycol(part):
        return lambda b, c: (b, 0, (2 * (c // cpg) + part) * cpg + c % cpg)

    tok = lambda imap: pl.BlockSpec((None, t_all, tc), imap)
    full = lambda a: pl.BlockSpec(a.shape, lambda b, c: (0,) * a.ndim, pipeline_mode=pl.Buffered(1))
    tabs = [jnp.asarray(stage1, dtype=_BF16), jnp.asarray(twc, dtype=_F32), jnp.asarray(tws, dtype=_F32),
            jnp.asarray(stage2, dtype=_BF16), jnp.asarray(ctx_tab, dtype=_BF16)]
    by_t1 = pltpu.VMEM((n1, n2 // SUBLANES, SUBLANES, tc), _F32)
    by_t2 = pltpu.VMEM((n2, n1 // SUBLANES, SUBLANES, tc), _F32)
    return pl.pallas_call(
        functools.partial(_time_dft_kernel, seq=seq, n1=n1, n2=n2),
        grid=(bsz, e // tc),
        in_specs=[tok(ycol(0)), tok(ycol(1)), tok(lambda b, c: (b, 0, e // tc + c))] + [full(t) for t in tabs],
        out_specs=tok(lambda b, c: (b, 0, c)),
        out_shape=jax.ShapeDtypeStruct((bsz, t_all, e), _BF16),
        scratch_shapes=[by_t1, by_t1, by_t2, by_t2, by_t2],
        compiler_params=_cparams("parallel", "parallel"),
        name="dft_time",
    )(y, y, uz, *tabs)


def kernel(x, c, ctx, c_ctx, ada_w, ada_b, norm_g, final_g, a_w_in, a_conv_w, a_conv_b, a_wq, a_wk, a_wv,
           a_w_ig, a_b_ig, a_w_fg, a_b_fg, a_hnorm_w, a_skip, a_w_out, b_w_in, b_w_grp, b_scale, b_w_out,
           c_w_in, c_w_grp, c_w_out):
    bsz, seq, d = x.shape
    depth = ada_w.shape[0]
    n_rows = 8
    assert bsz + 1 <= n_rows
    rows = jnp.concatenate([c, c_ctx[None], jnp.zeros((n_rows - bsz - 1, d), c.dtype)], axis=0)
    mods = _ada(rows, ada_w, ada_b).reshape(depth, n_rows, 1, 3 * d)
    xs = _embed(x, ctx)
    for i in range(depth):
        kind = i % N_MIXERS
        j = i // N_MIXERS
        h = _prenorm(xs, norm_g[i], mods[i], seq)
        w_in, w_out = ((a_w_in, a_w_out), (b_w_in, b_w_out), (c_w_in, c_w_out))[kind]
        uz = _in_proj(h, w_in, j)
        if kind == 0:
            a = _mlstm_branch(uz, a_conv_w[j], a_conv_b[j], a_wq[j], a_wk[j], a_wv[j],
                              a_w_ig[j], a_b_ig[j], a_w_fg[j], a_b_fg[j], a_hnorm_w[j], a_skip[j], seq)
        elif kind == 1:
            a = _pool_branch(uz, b_w_grp[j], b_scale[j], seq)
        else:
            a = _fourier_branch(uz, c_w_grp[j], seq)
        xs = _out_proj(a, w_out, j, xs, mods[i], seq)
    return _final_norm(xs, final_g, seq)
```

```python
import functools
import math

import jax
import jax.numpy as jnp
import numpy as np
from jax import lax
from jax.experimental import pallas as pl
from jax.experimental.pallas import tpu as pltpu

GRID_W = 64
N_MIXERS = 3
MLSTM_HEADS = 8
QKV_BLOCK = 4
CONV_W = 4
POOL_WINDOWS = (2, 4, 8, 16)
N_GROUPS = 4
POS_BASE = 10000.0
EPS = 1e-6

LANES = 128
SUBLANES = 8
VMEM_LIMIT_BYTES = 56 * 1024 * 1024
SCAN_CHUNK = 256
SCAN_HEADS_PER_STEP = 2

_F32 = jnp.float32
_BF16 = jnp.bfloat16


def _cparams(*sem):
    return pltpu.CompilerParams(dimension_semantics=sem, vmem_limit_bytes=VMEM_LIMIT_BYTES)


def _tile(n, target, align):
    best = None
    for t in range(align, min(n, target) + 1, align):
        if n % t == 0:
            best = t
    return n if best is None else best


def _silu(v):
    half = 0.5 * v
    return half + half * jnp.tanh(half)


def _matmul(grid, a, a_spec, b, b_spec, out_shape, out_spec, extras=(), extra_specs=(), epilogue=None,
            name="matmul"):
    n_extra = len(extras)
    if epilogue is None:
        epilogue = lambda acc, pids: acc

    def body(*refs):
        a_ref, b_ref = refs[0], refs[1]
        ex = refs[2:2 + n_extra]
        o_ref = refs[2 + n_extra]
        pids = tuple(pl.program_id(ax) for ax in range(len(grid)))
        acc = jnp.dot(a_ref[...].astype(_BF16), b_ref[...].astype(_BF16), preferred_element_type=_F32)
        o_ref[...] = epilogue(acc, pids, *ex).astype(o_ref.dtype)

    return pl.pallas_call(
        body,
        grid=grid,
        in_specs=[a_spec, b_spec, *extra_specs],
        out_specs=out_spec,
        out_shape=out_shape,
        compiler_params=_cparams(*(("parallel",) * len(grid))),
        name=name,
    )(a, b, *extras)


def _proj(a, w, layer, tm, tn, out_dtype, name, extras=(), extra_blocks=(), epilogue=None):
    bsz, t_all, k = a.shape
    n = w.shape[2]
    nt, mt = n // tn, t_all // tm
    pieces = bsz * mt
    tkp = k // pieces
    assert tkp * pieces == k and tkp % 16 == 0
    n_extra = len(extras)
    if epilogue is None:
        epilogue = lambda acc, pids: acc

    def body(*refs):
        a_ref, w_ref = refs[0], refs[1]
        ex = refs[2:2 + n_extra]
        o_ref, w_bf = refs[2 + n_extra], refs[3 + n_extra]
        pids = tuple(pl.program_id(ax) for ax in range(3))
        jj = pids[0]
        piece = pids[1] * mt + pids[2]

        def stage():
            rows = pl.ds(pl.multiple_of(piece * tkp, tkp), tkp)
            w_bf[jj % 2, rows, :] = w_ref[...].astype(_BF16)

        @pl.when(jj == 0)
        def _():
            stage()

        @pl.when(jj > 0)
        def _():
            stage()
            acc = jnp.dot(a_ref[...], w_bf[(jj + 1) % 2], preferred_element_type=_F32)
            o_ref[...] = epilogue(acc, pids, *ex).astype(o_ref.dtype)

    def a_map(jj, b, i):
        return (jnp.where(jj > 0, b, 0), jnp.where(jj > 0, i, 0), 0)

    def o_map(jj, b, i):
        return (jnp.where(jj > 0, b, 0), jnp.where(jj > 0, i, 0), jnp.maximum(jj - 1, 0))

    def w_map(jj, b, i):
        last = jj == nt
        return (layer, jnp.where(last, pieces - 1, b * mt + i), jnp.minimum(jj, nt - 1))

    return pl.pallas_call(
        body,
        grid=(nt + 1, bsz, mt),
        in_specs=[pl.BlockSpec((None, tm, k), a_map),
                  pl.BlockSpec((None, tkp, tn), w_map),
                  *[spec(o_map) for spec in extra_blocks]],
        out_specs=pl.BlockSpec((None, tm, tn), o_map),
        out_shape=jax.ShapeDtypeStruct((bsz, t_all, n), out_dtype),
        scratch_shapes=[pltpu.VMEM((2, k, tn), _BF16)],
        compiler_params=_cparams("arbitrary", "arbitrary", "arbitrary"),
        name=name,
    )(a, w, *extras)


def _ada_kernel(s_ref, w_ref, b_ref, o_ref):
    s = _silu(s_ref[...]).astype(_BF16)
    acc = jnp.dot(s, w_ref[...].astype(_BF16), preferred_element_type=_F32)
    o_ref[...] = acc + b_ref[...]


def _ada(rows, ada_w, ada_b):
    depth, d, n3 = ada_w.shape
    nr = rows.shape[0]
    tn = _tile(n3, 512, LANES)
    return pl.pallas_call(
        _ada_kernel,
        grid=(depth, n3 // tn),
        in_specs=[
            pl.BlockSpec((nr, d), lambda i, n: (0, 0)),
            pl.BlockSpec((None, d, tn), lambda i, n: (i, 0, n)),
            pl.BlockSpec((None, 1, tn), lambda i, n: (i, 0, n)),
        ],
        out_specs=pl.BlockSpec((None, nr, tn), lambda i, n: (i, 0, n)),
        out_shape=jax.ShapeDtypeStruct((depth, nr, n3), _F32),
        compiler_params=_cparams("parallel", "parallel"),
        name="ada",
    )(rows, ada_w, ada_b.reshape(depth, 1, n3))


def _embed_kernel(x_ref, c_ref, re_ref, ce_ref, o_ref, *, n_rows, half):
    r = pl.program_id(1)

    @pl.when(r < n_rows)
    def _():
        o_ref[:, :half] = x_ref[:, :half] + re_ref[...]
        o_ref[:, half:] = x_ref[:, half:] + ce_ref[...]

    @pl.when(r >= n_rows)
    def _():
        o_ref[...] = c_ref[...]


def _embed(x, ctx):
    bsz, seq, d = x.shape
    n_ctx = ctx.shape[1]
    n_rows = seq // GRID_W
    n_cblk = n_ctx // GRID_W
    half = d // 2
    quarter = d // 4
    omega = 1.0 / (POS_BASE ** (jnp.arange(quarter, dtype=_F32) / quarter))

    def axis_emb(p):
        ang = p[:, None] * omega[None, :]
        return jnp.concatenate([jnp.sin(ang), jnp.cos(ang)], axis=-1)

    row_emb = axis_emb(jnp.arange(n_rows, dtype=_F32)).reshape(n_rows, 1, half)
    col_emb = axis_emb(jnp.arange(GRID_W, dtype=_F32))
    return pl.pallas_call(
        functools.partial(_embed_kernel, n_rows=n_rows, half=half),
        grid=(bsz, n_rows + n_cblk),
        in_specs=[
            pl.BlockSpec((None, GRID_W, d), lambda b, r: (b, jnp.minimum(r, n_rows - 1), 0)),
            pl.BlockSpec((None, GRID_W, d), lambda b, r: (b, jnp.maximum(r - n_rows, 0), 0)),
            pl.BlockSpec((None, 1, half), lambda b, r: (jnp.minimum(r, n_rows - 1), 0, 0)),
            pl.BlockSpec((GRID_W, half), lambda b, r: (0, 0)),
        ],
        out_specs=pl.BlockSpec((None, GRID_W, d), lambda b, r: (b, r, 0)),
        out_shape=jax.ShapeDtypeStruct((bsz, seq + n_ctx, d), x.dtype),
        compiler_params=_cparams("parallel", "parallel"),
        name="embed",
    )(x, ctx, row_emb, col_emb)


def _prenorm_kernel(x_ref, g_ref, sh_ref, sc_ref, o_ref):
    x = x_ref[...]
    y = x * lax.rsqrt(jnp.mean(x * x, axis=-1, keepdims=True) + EPS)
    o_ref[...] = (y * g_ref[...] * (1.0 + sc_ref[...]) + sh_ref[...]).astype(o_ref.dtype)


def _prenorm(xs, g, mods, seq):
    bsz, t_all, d = xs.shape
    tb = _tile(math.gcd(seq, t_all - seq), 256, 8)
    n_lat = seq // tb

    def mrow(b, j):
        return jnp.where(j < n_lat, b, bsz)

    return pl.pallas_call(
        _prenorm_kernel,
        grid=(bsz, t_all // tb),
        in_specs=[
            pl.BlockSpec((None, tb, d), lambda b, j: (b, j, 0)),
            pl.BlockSpec((1, d), lambda b, j: (0, 0)),
            pl.BlockSpec((None, 1, d), lambda b, j: (mrow(b, j), 0, 0)),
            pl.BlockSpec((None, 1, d), lambda b, j: (mrow(b, j), 0, 1)),
        ],
        out_specs=pl.BlockSpec((None, tb, d), lambda b, j: (b, j, 0)),
        out_shape=jax.ShapeDtypeStruct((bsz, t_all, d), _BF16),
        compiler_params=_cparams("parallel", "parallel"),
        name="prenorm",
    )(xs, g.reshape(1, d), mods, mods)


def _final_norm_kernel(x_ref, g_ref, o_ref):
    x = x_ref[...]
    y = x * lax.rsqrt(jnp.mean(x * x, axis=-1, keepdims=True) + EPS)
    o_ref[...] = y * g_ref[...]


def _final_norm(xs, g, seq):
    bsz, t_all, d = xs.shape
    tb = _tile(math.gcd(seq, t_all - seq), 256, 8)
    return pl.pallas_call(
        _final_norm_kernel,
        grid=(bsz, seq // tb),
        in_specs=[
            pl.BlockSpec((None, tb, d), lambda b, j: (b, j, 0)),
            pl.BlockSpec((1, d), lambda b, j: (0, 0)),
        ],
        out_specs=pl.BlockSpec((None, tb, d), lambda b, j: (b, j, 0)),
        out_shape=jax.ShapeDtypeStruct((bsz, seq, d), xs.dtype),
        compiler_params=_cparams("parallel", "parallel"),
        name="final_norm",
    )(xs, g.reshape(1, d))


def _in_proj(h, w, layer):
    t_all, n = h.shape[1], w.shape[2]
    return _proj(h, w, layer, _tile(t_all, 1088, 16), _tile(n, 1024, LANES), _BF16, "in_proj")


def _out_proj(a, w, layer, xs, mods, seq):
    bsz, t_all, _ = a.shape
    d = w.shape[2]
    tm = _tile(t_all, 544, 16)
    tn = _tile(d, 512, LANES)
    g_off = 2 * (d // tn)

    def epilogue(acc, pids, res_ref, gl_ref, gc_ref):
        row = pids[2] * tm + lax.broadcasted_iota(jnp.int32, acc.shape, 0)
        gate = jnp.where(row < seq, gl_ref[...], gc_ref[...])
        return res_ref[...] + gate * acc

    def gate_block(row_of):
        return lambda om: pl.BlockSpec((None, 1, tn), lambda jj, b, i: (row_of(b), 0, g_off + om(jj, b, i)[2]))

    return _proj(a, w, layer, tm, tn, xs.dtype, "out_proj", extras=(xs, mods, mods),
                 extra_blocks=(lambda om: pl.BlockSpec((None, tm, tn), om),
                               gate_block(lambda b: b), gate_block(lambda b: bsz)),
                 epilogue=epilogue)


EDGE_ROWS = 16
EDGE_HALO = 16


def _segment_pos(shape, seq, n_ctx, row0):
    t = row0 + lax.broadcasted_iota(jnp.int32, shape, 0)
    is_lat = t < seq
    return jnp.where(is_lat, t, t - seq), jnp.where(is_lat, seq, n_ctx)


def _shift_rows(u, k, bounds=None):
    rolled = pltpu.roll(u, (-k) % u.shape[0], axis=0)
    if bounds is None:
        return rolled
    pos, seg_len = bounds
    valid = (pos >= -k) if k < 0 else (pos < seg_len - k)
    return jnp.where(valid, rolled, 0.0)


def _edge_windows(seq, t_all):
    p, h = EDGE_ROWS, EDGE_HALO
    assert seq >= p + h and t_all - seq >= 2 * (p + h)
    return ((0, p + h, 0, p), (seq - p - h, 2 * (p + h), h, 2 * p), (t_all - p - h, p + h, h, p))


def _redo_edges(fn, u_ref, dst_ref, seq, t_all):
    for row0, rows, off, n in _edge_windows(seq, t_all):
        uw = u_ref[row0:row0 + rows, :].astype(_F32)
        res = fn(uw, _segment_pos(uw.shape, seq, t_all - seq, row0))
        dst_ref[row0 + off:row0 + off + n, :] = res[off:off + n].astype(dst_ref.dtype)


def _mlstm_pre_kernel(u_ref, cw_ref, cb_ref, wq_ref, wk_ref, wv_ref, wg_ref, gb_ref,
                      xc_ref, q_ref, k_ref, v_ref, g_ref, acc_sc, *, seq, n_ctx, k_scale, n_gate):
    j = pl.program_id(1)
    ub = u_ref[...]
    left = (CONV_W - 1) // 2

    def conv(u, bounds):
        acc = cb_ref[...] + u * cw_ref[left:left + 1, :]
        for jj in range(CONV_W):
            if jj != left:
                acc = acc + _shift_rows(u, jj - left, bounds) * cw_ref[jj:jj + 1, :]
        return acc

    acc_sc[...] = conv(ub.astype(_F32), None)
    _redo_edges(conv, u_ref, acc_sc, seq, seq + n_ctx)
    xc = _silu(acc_sc[...])
    xcb = xc.astype(_BF16)
    te = ub.shape[1]
    own_block = (lax.broadcasted_iota(jnp.int32, (te, te), 0) // QKV_BLOCK
                 == lax.broadcasted_iota(jnp.int32, (te, te), 1) // QKV_BLOCK)
    dense = lambda w_ref: jnp.where(own_block, w_ref[...].astype(_F32), 0.0).astype(_BF16)
    q = jnp.dot(xcb, dense(wq_ref), preferred_element_type=_F32)
    k = jnp.dot(xcb, dense(wk_ref), preferred_element_type=_F32)
    v = jnp.dot(ub, dense(wv_ref), preferred_element_type=_F32)
    qb, kb, vb = q.astype(_BF16), k.astype(_BF16), v.astype(_BF16)
    xc_ref[...] = xcb
    q_ref[...] = qb
    k_ref[...] = (k * k_scale).astype(_BF16)
    v_ref[...] = vb
    part = jnp.dot(jnp.concatenate([qb, kb, vb], axis=1), wg_ref[...], preferred_element_type=_F32)

    @pl.when(j == 0)
    def _():
        g_ref[...] = part + gb_ref[...]

    @pl.when(j > 0)
    def _():
        g_ref[...] += part

    @pl.when(j == pl.num_programs(1) - 1)
    def _():
        g = g_ref[...]
        col = lax.broadcasted_iota(jnp.int32, g.shape, 1)
        log_sig = jnp.minimum(g, 0.0) - jnp.log1p(jnp.exp(-jnp.abs(g)))
        g_ref[...] = jnp.where(col < n_gate, g, log_sig)


def _blockdiag_rows(w, te):
    rows = w.reshape(w.shape[0] * QKV_BLOCK, QKV_BLOCK).astype(_BF16)
    return jnp.tile(rows, (1, te // QKV_BLOCK))


def _mlstm_pre(uz, conv_w, conv_b, wq, wk, wv, w_ig, b_ig, w_fg, b_fg, seq):
    bsz, t_all, e2 = uz.shape
    e = e2 // 2
    nh = w_ig.shape[-1]
    te = _tile(e, 256, LANES)
    nt = e // te
    ng = 4 * nh
    wg = jnp.concatenate([w_ig[0], w_ig[1], w_fg[0], w_fg[1]], axis=-1)
    wg = wg.reshape(3, nt, te, ng).transpose(1, 0, 2, 3).reshape(nt, 3 * te, ng).astype(_BF16)
    gb = jnp.concatenate([b_ig[0], b_ig[1], b_fg[0], b_fg[1]], axis=-1).reshape(1, ng)
    tok = pl.BlockSpec((None, t_all, te), lambda b, j: (b, 0, j))
    bd = pl.BlockSpec((te, te), lambda b, j: (j, 0))
    act = jax.ShapeDtypeStruct((bsz, t_all, e), _BF16)
    return pl.pallas_call(
        functools.partial(_mlstm_pre_kernel, seq=seq, n_ctx=t_all - seq,
                          k_scale=float((e // nh) ** -0.5), n_gate=2 * nh),
        grid=(bsz, nt),
        in_specs=[
            tok,
            pl.BlockSpec((CONV_W, te), lambda b, j: (0, j)),
            pl.BlockSpec((1, te), lambda b, j: (0, j)),
            bd, bd, bd,
            pl.BlockSpec((None, 3 * te, ng), lambda b, j: (j, 0, 0)),
            pl.BlockSpec((1, ng), lambda b, j: (0, 0)),
        ],
        out_specs=[tok, tok, tok, tok, pl.BlockSpec((None, t_all, ng), lambda b, j: (b, 0, 0))],
        out_shape=[act, act, act, act, jax.ShapeDtypeStruct((bsz, t_all, ng), _F32)],
        scratch_shapes=[pltpu.VMEM((t_all, te), _F32)],
        compiler_params=_cparams("parallel", "arbitrary"),
        name="mlstm_pre",
    )(uz, conv_w, conv_b.reshape(1, e), _blockdiag_rows(wq, te), _blockdiag_rows(wk, te),
      _blockdiag_rows(wv, te), wg, gb)


def _scan_kernel(q_ref, k_ref, v_ref, li_ref, lf_ref, *rest, rev):
    if rev:
        hf_ref, xc_ref, z_ref, nw_ref, sk_ref, o_ref, c_sc, cb_sc, n_sc, m_sc = rest
    else:
        o_ref, c_sc, cb_sc, n_sc, m_sc = rest

    @pl.when(pl.program_id(2) == 0)
    def _():
        c_sc[...] = jnp.zeros(c_sc.shape, _F32)
        cb_sc[...] = jnp.zeros(cb_sc.shape, _BF16)
        n_sc[...] = jnp.zeros(n_sc.shape, _F32)
        m_sc[...] = jnp.zeros(m_sc.shape, _F32)

    ln = q_ref.shape[0]
    dh = c_sc.shape[-1]
    ti = lax.broadcasted_iota(jnp.int32, (ln, ln), 0)
    si = lax.broadcasted_iota(jnp.int32, (ln, ln), 1)
    incl = si >= ti if rev else si <= ti
    incl_t = ti >= si if rev else ti <= si
    eye = si == ti
    for hh in range(c_sc.shape[0]):
        cols = slice(hh * dh, (hh + 1) * dh)
        h = _scan_chunk(q_ref[:, cols], k_ref[:, cols], v_ref[:, cols], li_ref[hh], lf_ref[hh],
                        incl, incl_t, eye, c_sc.at[hh], cb_sc.at[hh], n_sc.at[hh], m_sc.at[hh])
        if rev:
            h = h + hf_ref[:, cols].astype(_F32)
            mu = jnp.mean(h, axis=-1, keepdims=True)
            hc = h - mu
            var = jnp.mean(hc * hc, axis=-1, keepdims=True)
            hn = hc * lax.rsqrt(var + EPS) * nw_ref[:, cols]
            h = (hn + sk_ref[:, cols] * xc_ref[:, cols].astype(_F32)) * _silu(z_ref[:, cols].astype(_F32))
        o_ref[:, cols] = h.astype(o_ref.dtype)


def _scan_chunk(q, k, v, li_r, lf_r, incl, incl_t, eye, c_sc, cb_sc, n_sc, m_sc):
    ln = q.shape[0]
    lf_b = jnp.broadcast_to(lf_r, (ln, ln))
    li_b = jnp.broadcast_to(li_r, (ln, ln))
    b_col = jnp.sum(jnp.where(incl, lf_b, 0.0), axis=1, keepdims=True)
    lf_col = jnp.sum(jnp.where(eye, lf_b, 0.0), axis=1, keepdims=True)
    li_col = jnp.sum(jnp.where(eye, li_b, 0.0), axis=1, keepdims=True)
    b_row = jnp.sum(jnp.where(incl_t, lf_col, 0.0), axis=0, keepdims=True)
    b_end = jnp.sum(lf_r, axis=1, keepdims=True)
    m_old = m_sc[...]
    g_row = b_end - b_row + li_r
    g_col = b_end - b_col + li_col
    m_new = jnp.maximum(b_end + m_old, jnp.max(g_row, axis=1, keepdims=True))
    wg_col = jnp.exp(g_col - m_new)
    decay = jnp.exp(b_end + m_old - m_new)

    logw = jnp.where(incl, b_col - b_row + li_r, -jnp.inf)
    inter = b_col + m_old
    m_t = jnp.maximum(jnp.max(logw, axis=1, keepdims=True), inter)
    s = lax.dot_general(q, k, (((1,), (1,)), ((), ())), preferred_element_type=_F32) * jnp.exp(logw - m_t)
    w_inter = jnp.exp(inter - m_t)
    den = (jnp.sum(s, axis=1, keepdims=True)
           + w_inter * jnp.sum(q.astype(_F32) * n_sc[...], axis=1, keepdims=True))
    inv = 1.0 / jnp.maximum(jnp.abs(den), jnp.exp(-m_t))
    sb = s.astype(_BF16)
    kw = k.astype(_F32) * wg_col
    kwb = kw.astype(_BF16)
    n_sc[...] = decay * n_sc[...] + jnp.sum(kw, axis=0, keepdims=True)
    m_sc[...] = m_new
    dv = v.shape[1]
    step = min(dv, 2 * LANES)
    pieces = []
    for c0 in range(0, dv, step):
        cols = slice(c0, c0 + step)
        vc = v[:, cols]
        num = (jnp.dot(sb, vc, preferred_element_type=_F32)
               + w_inter * jnp.dot(q, cb_sc[:, cols], preferred_element_type=_F32))
        pieces.append(num * inv)
        c_new = decay * c_sc[:, cols] + lax.dot_general(kwb, vc, (((0,), (0,)), ((), ())),
                                                        preferred_element_type=_F32)
        c_sc[:, cols] = c_new
        cb_sc[:, cols] = c_new.astype(_BF16)
    return jnp.concatenate(pieces, axis=1)


def _mlstm_scan(q, k, v, gates, uz, xc, hnorm_w, skip, seq, nh):
    bsz, t_all, e = q.shape
    dh = e // nh
    n_ctx = t_all - seq
    ln = _tile(math.gcd(seq, n_ctx), SCAN_CHUNK, 8)
    nc = t_all // ln
    ncl = seq // ln
    ncc = n_ctx // ln
    g = gates.reshape(bsz, nc, ln, 2, 2, nh).transpose(3, 4, 0, 5, 1, 2).reshape(2, 2, bsz, nh, nc, 1, ln)
    hb = SCAN_HEADS_PER_STEP if nh % SCAN_HEADS_PER_STEP == 0 else 1
    act = jax.ShapeDtypeStruct((bsz, t_all, e), _BF16)
    scratch = [pltpu.VMEM((hb, dh, dh), _F32), pltpu.VMEM((hb, dh, dh), _BF16),
               pltpu.VMEM((hb, 1, dh), _F32), pltpu.VMEM((hb, 1, 1), _F32)]

    def run(rev, extra_in, extra_specs):
        def chunk(s):
            if rev:
                return jnp.where(s < ncc, ncl + ncc - 1 - s, ncl - 1 - (s - ncc))
            return jnp.where(s < ncc, ncl + s, s - ncc)

        tok = lambda col0: pl.BlockSpec((None, ln, hb * dh), lambda b, hh, s: (b, chunk(s), col0 + hh))
        gate = pl.BlockSpec((None, hb, None, 1, ln), lambda b, hh, s: (b, hh, chunk(s), 0, 0))
        return pl.pallas_call(
            functools.partial(_scan_kernel, rev=rev),
            grid=(bsz, nh // hb, nc),
            in_specs=[tok(0), tok(0), tok(0), gate, gate, *extra_specs(tok)],
            out_specs=tok(0),
            out_shape=act,
            scratch_shapes=scratch,
            compiler_params=_cparams("parallel", "parallel", "arbitrary"),
            name="mlstm_scan_rev" if rev else "mlstm_scan_fwd",
        )(q, k, v, g[0, int(rev)], g[1, int(rev)], *extra_in)

    h_fwd = run(False, (), lambda tok: ())
    vec = pl.BlockSpec((1, hb * dh), lambda b, hh, s: (0, hh))
    return run(True, (h_fwd, xc, uz, hnorm_w.reshape(1, e), skip.reshape(1, e)),
               lambda tok: (tok(0), tok(0), tok(nh // hb), vec, vec))


def _mlstm_branch(uz, conv_w, conv_b, wq, wk, wv, w_ig, b_ig, w_fg, b_fg, hnorm_w, skip, seq):
    nh = w_ig.shape[-1]
    xc, q, k, v, gates = _mlstm_pre(uz, conv_w, conv_b, wq, wk, wv, w_ig, b_ig, w_fg, b_fg, seq)
    return _mlstm_scan(q, k, v, gates, uz, xc, hnorm_w, skip, seq, nh)


def _pool_kernel(u_ref, o_ref, *, seq, n_ctx, tiles_per_group):
    grp = pl.program_id(1) // tiles_per_group
    for gi, w in enumerate(POOL_WINDOWS):
        lo = w // 2
        hi = w - 1 - lo
        assert hi == lo - 1 and lo <= EDGE_ROWS

        def delta(u, bounds, w=w, lo=lo, hi=hi):
            trail = u
            lead = u
            span = 1
            while span < lo:
                trail = trail + _shift_rows(trail, -span, bounds)
                lead = lead + _shift_rows(lead, span, bounds)
                span *= 2
            win = _shift_rows(trail, -1, bounds) + lead
            if bounds is None:
                return win * (1.0 / w) - u
            pos, seg_len = bounds
            cnt = jnp.minimum(pos + hi + 1, seg_len) - jnp.maximum(pos - lo, 0)
            return win / cnt.astype(_F32) - u

        @pl.when(grp == gi)
        def _(delta=delta):
            o_ref[...] = delta(u_ref[...].astype(_F32), None).astype(o_ref.dtype)
            _redo_edges(delta, u_ref, o_ref, seq, seq + n_ctx)


def _pool_group_linear(a, w_grp, uz, scale):
    bsz, t_all, e = a.shape
    ng, gw, _ = w_grp.shape
    tm = _tile(t_all, 1088, 16)
    tn = _tile(gw, 1024, LANES)
    npg = gw // tn

    def epilogue(acc, pids, z_ref, sc_ref):
        return acc * sc_ref[...] * _silu(z_ref[...].astype(_F32))

    extras = [uz, scale.reshape(1, e)]
    especs = [pl.BlockSpec((None, tm, tn), lambda b, g, j, i, k: (b, i, (e // tn) + g * npg + j)),
              pl.BlockSpec((1, tn), lambda b, g, j, i, k: (0, g * npg + j))]
    return _matmul(
        (bsz, ng, npg, t_all // tm, 1),
        a, pl.BlockSpec((None, tm, gw), lambda b, g, j, i, k: (b, i, g)),
        w_grp, pl.BlockSpec((None, gw, tn), lambda b, g, j, i, k: (g, 0, j)),
        jax.ShapeDtypeStruct((bsz, t_all, e), _BF16),
        pl.BlockSpec((None, tm, tn), lambda b, g, j, i, k: (b, i, g * npg + j)),
        extras=tuple(extras), extra_specs=tuple(especs), epilogue=epilogue, name="pool_group")


def _pool_branch(uz, w_grp, scale, seq):
    bsz, t_all, e2 = uz.shape
    e = e2 // 2
    te = _tile(e // N_GROUPS, 256, LANES)
    tok = pl.BlockSpec((None, t_all, te), lambda b, j: (b, 0, j))
    dlt = pl.pallas_call(
        functools.partial(_pool_kernel, seq=seq, n_ctx=t_all - seq, tiles_per_group=e // N_GROUPS // te),
        grid=(bsz, e // te),
        in_specs=[tok],
        out_specs=tok,
        out_shape=jax.ShapeDtypeStruct((bsz, t_all, e), _BF16),
        compiler_params=_cparams("parallel", "parallel"),
        name="pool",
    )(uz)
    return _pool_group_linear(dlt, w_grp, uz, scale)


def _cos_sin(n_out, n_in, period):
    ang = (np.outer(np.arange(n_out), np.arange(n_in)) % period) * (2.0 * np.pi / period)
    return np.cos(ang), np.sin(ang)


@functools.lru_cache(maxsize=None)
def _chan_table(gw):
    c, s = _cos_sin(gw, gw, gw)
    return np.concatenate([c, s], axis=0) * gw ** -0.5


@functools.lru_cache(maxsize=None)
def _time_tables(seq, n_ctx):
    n1 = max(f for f in range(1, math.isqrt(seq) + 1) if seq % f == 0)
    n2 = seq // n1
    g = SUBLANES
    assert n1 % g == 0 and n2 % g == 0
    eye = np.eye(g)
    c1, s1 = _cos_sin(n1, n1, n1)
    base = np.block([[c1, -s1], [-s1, -c1]]) * seq ** -0.5
    stage1 = np.einsum('ab,rkst->arkstb', eye, base.reshape(2, n1, 2, n1)).reshape(g * 2 * n1, 2 * n1 * g)
    twc, tws = _cos_sin(n2, n1, seq)
    c2, s2 = _cos_sin(n2, n2, n2)
    cs2 = np.concatenate([c2, s2], axis=1)
    stage2 = np.einsum('ab,krt->kartb', eye, cs2.reshape(n2, 2, n2)).reshape(n2 * g, 2 * n2 * g)
    cc, sc = _cos_sin(n_ctx, n_ctx, n_ctx)
    ctx_tab = np.concatenate([cc, -sc], axis=1) * n_ctx ** -0.5
    lanes = lambda a: np.repeat(a[:, :, None], LANES, axis=2)
    return n1, n2, stage1, lanes(twc), lanes(tws), stage2, ctx_tab


def _time_dft_kernel(y1_ref, y2_ref, z_ref, w1_ref, twc_ref, tws_ref, w2_ref, wc_ref, o_ref,
                     y1f, y2f, ar, ai, out, *, seq, n1, n2):
    g = SUBLANES
    tc = o_ref.shape[-1]
    wide = lambda tab: jnp.concatenate([tab] * (tc // LANES), axis=1)
    y1f[...] = y1_ref[:seq, :].astype(_F32).reshape(y1f.shape)
    y2f[...] = y2_ref[:seq, :].astype(_F32).reshape(y2f.shape)
    for th in range(n2 // g):
        rhs = jnp.concatenate([y1f[:, th].reshape(n1 * g, tc), y2f[:, th].reshape(n1 * g, tc)], axis=0)
        res = jnp.dot(w1_ref[...], rhs.astype(_BF16), preferred_element_type=_F32)
        for tl in range(g):
            t2 = th * g + tl
            a_r = res[tl * 2 * n1:tl * 2 * n1 + n1]
            a_i = res[tl * 2 * n1 + n1:(tl + 1) * 2 * n1]
            c, s = wide(twc_ref[t2]), wide(tws_ref[t2])
            ar[t2] = (a_r * c + a_i * s).reshape(ar.shape[1:])
            ai[t2] = (a_i * c - a_r * s).reshape(ai.shape[1:])
    for kh in range(n1 // g):
        rhs = jnp.concatenate([ar[:, kh].reshape(n2 * g, tc), ai[:, kh].reshape(n2 * g, tc)], axis=0)
        res = jnp.dot(w2_ref[...], rhs.astype(_BF16), preferred_element_type=_F32)
        out[:, kh] = res.reshape(n2, g, tc)
    gate = _silu(z_ref[:seq, :].astype(_F32))
    o_ref[:seq, :] = (out[...].reshape(seq, tc) * gate).astype(o_ref.dtype)
    rhs = jnp.concatenate([y1_ref[seq:, :], y2_ref[seq:, :]], axis=0)
    res = jnp.dot(wc_ref[...], rhs, preferred_element_type=_F32)
    o_ref[seq:, :] = (res * _silu(z_ref[seq:, :].astype(_F32))).astype(o_ref.dtype)


def _fourier_branch(uz, w_grp, seq):
    bsz, t_all, e2 = uz.shape
    n_ctx = t_all - seq
    e = e2 // 2
    gw = e // N_GROUPS
    tab = jnp.asarray(_chan_table(gw), dtype=_BF16)
    tw = _tile(gw, 1024, LANES)
    nrow = gw // tw
    wcomb = _matmul(
        (N_GROUPS, gw // tw, 2 * nrow, 1),
        tab, pl.BlockSpec((tw, gw), lambda g, j, i, k: (i, 0)),
        w_grp, pl.BlockSpec((None, gw, tw), lambda g, j, i, k: (g, 0, j)),
        jax.ShapeDtypeStruct((N_GROUPS, gw, 2 * gw), _BF16),
        pl.BlockSpec((None, tw, tw), lambda g, j, i, k: (g, i % nrow, (i // nrow) * (gw // tw) + j)),
        name="fourier_weight")
    tm = _tile(t_all, 1088, 16)
    tn = _tile(gw, 1024, LANES)
    y = _matmul(
        (bsz, N_GROUPS, 2 * gw // tn, t_all // tm, 1),
        uz, pl.BlockSpec((None, tm, gw), lambda b, g, j, i, k: (b, i, g)),
        wcomb, pl.BlockSpec((None, gw, tn), lambda b, g, j, i, k: (g, 0, j)),
        jax.ShapeDtypeStruct((bsz, t_all, 2 * e), _BF16),
        pl.BlockSpec((None, tm, tn), lambda b, g, j, i, k: (b, i, g * (2 * gw // tn) + j)),
        name="dft_chan")
    n1, n2, stage1, twc, tws, stage2, ctx_tab = _time_tables(seq, n_ctx)
    tc = _tile(gw, 256, LANES)
    cpg = gw // tc

    def ycol(part):
        return lambda b, c: (b, 0, (2 * (c // cpg) + part) * cpg + c % cpg)

    tok = lambda imap: pl.BlockSpec((None, t_all, tc), imap)
    full = lambda a: pl.BlockSpec(a.shape, lambda b, c: (0,) * a.ndim, pipeline_mode=pl.Buffered(1))
    tabs = [jnp.asarray(stage1, dtype=_BF16), jnp.asarray(twc, dtype=_F32), jnp.asarray(tws, dtype=_F32),
            jnp.asarray(stage2, dtype=_BF16), jnp.asarray(ctx_tab, dtype=_BF16)]
    by_t1 = pltpu.VMEM((n1, n2 // SUBLANES, SUBLANES, tc), _F32)
    by_t2 = pltpu.VMEM((n2, n1 // SUBLANES, SUBLANES, tc), _F32)
    return pl.pallas_call(
        functools.partial(_time_dft_kernel, seq=seq, n1=n1, n2=n2),
        grid=(bsz, e // tc),
        in_specs=[tok(ycol(0)), tok(ycol(1)), tok(lambda b, c: (b, 0, e // tc + c))] + [full(t) for t in tabs],
        out_specs=tok(lambda b, c: (b, 0, c)),
        out_shape=jax.ShapeDtypeStruct((bsz, t_all, e), _BF16),
        scratch_shapes=[by_t1, by_t1, by_t2, by_t2, by_t2],
        compiler_params=_cparams("parallel", "parallel"),
        name="dft_time",
    )(y, y, uz, *tabs)


def kernel(x, c, ctx, c_ctx, ada_w, ada_b, norm_g, final_g, a_w_in, a_conv_w, a_conv_b, a_wq, a_wk, a_wv,
           a_w_ig, a_b_ig, a_w_fg, a_b_fg, a_hnorm_w, a_skip, a_w_out, b_w_in, b_w_grp, b_scale, b_w_out,
           c_w_in, c_w_grp, c_w_out):
    bsz, seq, d = x.shape
    depth = ada_w.shape[0]
    n_rows = 8
    assert bsz + 1 <= n_rows
    rows = jnp.concatenate([c, c_ctx[None], jnp.zeros((n_rows - bsz - 1, d), c.dtype)], axis=0)
    mods = _ada(rows, ada_w, ada_b).reshape(depth, n_rows, 1, 3 * d)
    xs = _embed(x, ctx)
    for i in range(depth):
        kind = i % N_MIXERS
        j = i // N_MIXERS
        h = _prenorm(xs, norm_g[i], mods[i], seq)
        w_in, w_out = ((a_w_in, a_w_out), (b_w_in, b_w_out), (c_w_in, c_w_out))[kind]
        uz = _in_proj(h, w_in, j)
        if kind == 0:
            a = _mlstm_branch(uz, a_conv_w[j], a_conv_b[j], a_wq[j], a_wk[j], a_wv[j],
                              a_w_ig[j], a_b_ig[j], a_w_fg[j], a_b_fg[j], a_hnorm_w[j], a_skip[j], seq)
        elif kind == 1:
            a = _pool_branch(uz, b_w_grp[j], b_scale[j], seq)
        else:
            a = _fourier_branch(uz, c_w_grp[j], seq)
        xs = _out_proj(a, w_out, j, xs, mods[i], seq)
    return _final_norm(xs, final_g, seq)
```

```python
import functools
import math

import jax
import jax.numpy as jnp
import numpy as np
from jax import lax
from jax.experimental import pallas as pl
from jax.experimental.pallas import tpu as pltpu

GRID_W = 64
N_MIXERS = 3
MLSTM_HEADS = 8
QKV_BLOCK = 4
CONV_W = 4
POOL_WINDOWS = (2, 4, 8, 16)
N_GROUPS = 4
POS_BASE = 10000.0
EPS = 1e-6

LANES = 128
SUBLANES = 8
VMEM_LIMIT_BYTES = 56 * 1024 * 1024
SCAN_CHUNK = 256
SCAN_HEADS_PER_STEP = (4, 2)

_F32 = jnp.float32
_BF16 = jnp.bfloat16


def _cparams(*sem):
    return pltpu.CompilerParams(dimension_semantics=sem, vmem_limit_bytes=VMEM_LIMIT_BYTES)


def _tile(n, target, align):
    best = None
    for t in range(align, min(n, target) + 1, align):
        if n % t == 0:
            best = t
    return n if best is None else best


def _silu(v):
    half = 0.5 * v
    return half + half * jnp.tanh(half)


def _matmul(grid, a, a_spec, b, b_spec, out_shape, out_spec, extras=(), extra_specs=(), epilogue=None,
            name="matmul"):
    n_extra = len(extras)
    if epilogue is None:
        epilogue = lambda acc, pids: acc

    def body(*refs):
        a_ref, b_ref = refs[0], refs[1]
        ex = refs[2:2 + n_extra]
        o_ref = refs[2 + n_extra]
        pids = tuple(pl.program_id(ax) for ax in range(len(grid)))
        acc = jnp.dot(a_ref[...].astype(_BF16), b_ref[...].astype(_BF16), preferred_element_type=_F32)
        o_ref[...] = epilogue(acc, pids, *ex).astype(o_ref.dtype)

    return pl.pallas_call(
        body,
        grid=grid,
        in_specs=[a_spec, b_spec, *extra_specs],
        out_specs=out_spec,
        out_shape=out_shape,
        compiler_params=_cparams(*(("parallel",) * len(grid))),
        name=name,
    )(a, b, *extras)


def _proj(a, w, layer, tm, tn, out_dtype, name, extras=(), extra_blocks=(), epilogue=None):
    bsz, t_all, k = a.shape
    n = w.shape[2]
    nt, mt = n // tn, t_all // tm
    pieces = bsz * mt
    tkp = k // pieces
    assert tkp * pieces == k and tkp % 16 == 0
    n_extra = len(extras)
    if epilogue is None:
        epilogue = lambda acc, pids: acc

    def body(*refs):
        a_ref, w_ref = refs[0], refs[1]
        ex = refs[2:2 + n_extra]
        o_ref, w_bf = refs[2 + n_extra], refs[3 + n_extra]
        pids = tuple(pl.program_id(ax) for ax in range(3))
        jj = pids[0]
        piece = pids[1] * mt + pids[2]

        def stage():
            rows = pl.ds(pl.multiple_of(piece * tkp, tkp), tkp)
            w_bf[jj % 2, rows, :] = w_ref[...].astype(_BF16)

        @pl.when(jj == 0)
        def _():
            stage()

        @pl.when(jj > 0)
        def _():
            stage()
            acc = jnp.dot(a_ref[...], w_bf[(jj + 1) % 2], preferred_element_type=_F32)
            o_ref[...] = epilogue(acc, pids, *ex).astype(o_ref.dtype)

    def a_map(jj, b, i):
        return (jnp.where(jj > 0, b, 0), jnp.where(jj > 0, i, 0), 0)

    def o_map(jj, b, i):
        return (jnp.where(jj > 0, b, 0), jnp.where(jj > 0, i, 0), jnp.maximum(jj - 1, 0))

    def w_map(jj, b, i):
        last = jj == nt
        return (layer, jnp.where(last, pieces - 1, b * mt + i), jnp.minimum(jj, nt - 1))

    return pl.pallas_call(
        body,
        grid=(nt + 1, bsz, mt),
        in_specs=[pl.BlockSpec((None, tm, k), a_map),
                  pl.BlockSpec((None, tkp, tn), w_map),
                  *[spec(o_map) for spec in extra_blocks]],
        out_specs=pl.BlockSpec((None, tm, tn), o_map),
        out_shape=jax.ShapeDtypeStruct((bsz, t_all, n), out_dtype),
        scratch_shapes=[pltpu.VMEM((2, k, tn), _BF16)],
        compiler_params=_cparams("arbitrary", "arbitrary", "arbitrary"),
        name=name,
    )(a, w, *extras)


def _ada_kernel(s_ref, w_ref, b_ref, o_ref):
    s = _silu(s_ref[...]).astype(_BF16)
    acc = jnp.dot(s, w_ref[...].astype(_BF16), preferred_element_type=_F32)
    o_ref[...] = acc + b_ref[...]


def _ada(rows, ada_w, ada_b):
    depth, d, n3 = ada_w.shape
    nr = rows.shape[0]
    tn = _tile(n3, 1024, LANES)
    return pl.pallas_call(
        _ada_kernel,
        grid=(depth, n3 // tn),
        in_specs=[
            pl.BlockSpec((nr, d), lambda i, n: (0, 0)),
            pl.BlockSpec((None, d, tn), lambda i, n: (i, 0, n)),
            pl.BlockSpec((None, 1, tn), lambda i, n: (i, 0, n)),
        ],
        out_specs=pl.BlockSpec((None, nr, tn), lambda i, n: (i, 0, n)),
        out_shape=jax.ShapeDtypeStruct((depth, nr, n3), _F32),
        compiler_params=_cparams("parallel", "parallel"),
        name="ada",
    )(rows, ada_w, ada_b.reshape(depth, 1, n3))


def _embed_kernel(x_ref, c_ref, re_ref, ce_ref, o_ref, *, n_lat, half):
    r = pl.program_id(1)

    @pl.when(r < n_lat)
    def _():
        for i in range(re_ref.shape[0]):
            rows = slice(i * GRID_W, (i + 1) * GRID_W)
            o_ref[rows, :half] = x_ref[rows, :half] + re_ref[i:i + 1, :]
            o_ref[rows, half:] = x_ref[rows, half:] + ce_ref[...]

    @pl.when(r >= n_lat)
    def _():
        o_ref[...] = c_ref[...]


def _embed(x, ctx):
    bsz, seq, d = x.shape
    n_ctx = ctx.shape[1]
    n_rows = seq // GRID_W
    rb = max(f for f in (4, 2, 1) if n_rows % f == 0 and n_ctx % (f * GRID_W) == 0)
    tb = rb * GRID_W
    n_lat = seq // tb
    half = d // 2
    quarter = d // 4
    omega = 1.0 / (POS_BASE ** (jnp.arange(quarter, dtype=_F32) / quarter))

    def axis_emb(p):
        ang = p[:, None] * omega[None, :]
        return jnp.concatenate([jnp.sin(ang), jnp.cos(ang)], axis=-1)

    row_emb = axis_emb(jnp.arange(n_rows, dtype=_F32)).reshape(n_lat, rb, half)
    col_emb = axis_emb(jnp.arange(GRID_W, dtype=_F32))
    return pl.pallas_call(
        functools.partial(_embed_kernel, n_lat=n_lat, half=half),
        grid=(bsz, n_lat + n_ctx // tb),
        in_specs=[
            pl.BlockSpec((None, tb, d), lambda b, r: (b, jnp.minimum(r, n_lat - 1), 0)),
            pl.BlockSpec((None, tb, d), lambda b, r: (b, jnp.maximum(r - n_lat, 0), 0)),
            pl.BlockSpec((None, rb, half), lambda b, r: (jnp.minimum(r, n_lat - 1), 0, 0)),
            pl.BlockSpec((GRID_W, half), lambda b, r: (0, 0)),
        ],
        out_specs=pl.BlockSpec((None, tb, d), lambda b, r: (b, r, 0)),
        out_shape=jax.ShapeDtypeStruct((bsz, seq + n_ctx, d), x.dtype),
        compiler_params=_cparams("parallel", "parallel"),
        name="embed",
    )(x, ctx, row_emb, col_emb)


def _prenorm_kernel(x_ref, g_ref, sh_ref, sc_ref, o_ref):
    x = x_ref[...]
    y = x * lax.rsqrt(jnp.mean(x * x, axis=-1, keepdims=True) + EPS)
    o_ref[...] = (y * g_ref[...] * (1.0 + sc_ref[...]) + sh_ref[...]).astype(o_ref.dtype)


def _prenorm(xs, g, mods, seq):
    bsz, t_all, d = xs.shape
    tb = _tile(math.gcd(seq, t_all - seq), 256, 8)
    n_lat = seq // tb

    def mrow(b, j):
        return jnp.where(j < n_lat, b, bsz)

    return pl.pallas_call(
        _prenorm_kernel,
        grid=(bsz, t_all // tb),
        in_specs=[
            pl.BlockSpec((None, tb, d), lambda b, j: (b, j, 0)),
            pl.BlockSpec((1, d), lambda b, j: (0, 0)),
            pl.BlockSpec((None, 1, d), lambda b, j: (mrow(b, j), 0, 0)),
            pl.BlockSpec((None, 1, d), lambda b, j: (mrow(b, j), 0, 1)),
        ],
        out_specs=pl.BlockSpec((None, tb, d), lambda b, j: (b, j, 0)),
        out_shape=jax.ShapeDtypeStruct((bsz, t_all, d), _BF16),
        compiler_params=_cparams("parallel", "parallel"),
        name="prenorm",
    )(xs, g.reshape(1, d), mods, mods)


def _final_norm_kernel(x_ref, g_ref, o_ref):
    x = x_ref[...]
    y = x * lax.rsqrt(jnp.mean(x * x, axis=-1, keepdims=True) + EPS)
    o_ref[...] = y * g_ref[...]


def _final_norm(xs, g, seq):
    bsz, t_all, d = xs.shape
    tb = _tile(seq, 512, 8)
    return pl.pallas_call(
        _final_norm_kernel,
        grid=(bsz, seq // tb),
        in_specs=[
            pl.BlockSpec((None, tb, d), lambda b, j: (b, j, 0)),
            pl.BlockSpec((1, d), lambda b, j: (0, 0)),
        ],
        out_specs=pl.BlockSpec((None, tb, d), lambda b, j: (b, j, 0)),
        out_shape=jax.ShapeDtypeStruct((bsz, seq, d), xs.dtype),
        compiler_params=_cparams("parallel", "parallel"),
        name="final_norm",
    )(xs, g.reshape(1, d))


def _in_proj(h, w, layer):
    t_all, n = h.shape[1], w.shape[2]
    return _proj(h, w, layer, _tile(t_all, 1088, 16), _tile(n, 1024, LANES), _BF16, "in_proj")


def _out_proj(a, w, layer, xs, mods, seq):
    bsz, t_all, _ = a.shape
    d = w.shape[2]
    tm = _tile(t_all, 544, 16)
    tn = _tile(d, 512, LANES)
    g_off = 2 * (d // tn)

    def epilogue(acc, pids, res_ref, gl_ref, gc_ref):
        row = pids[2] * tm + lax.broadcasted_iota(jnp.int32, acc.shape, 0)
        gate = jnp.where(row < seq, gl_ref[...], gc_ref[...])
        return res_ref[...] + gate * acc

    def gate_block(row_of):
        return lambda om: pl.BlockSpec((None, 1, tn), lambda jj, b, i: (row_of(b), 0, g_off + om(jj, b, i)[2]))

    return _proj(a, w, layer, tm, tn, xs.dtype, "out_proj", extras=(xs, mods, mods),
                 extra_blocks=(lambda om: pl.BlockSpec((None, tm, tn), om),
                               gate_block(lambda b: b), gate_block(lambda b: bsz)),
                 epilogue=epilogue)


EDGE_ROWS = 16
EDGE_HALO = 16


def _segment_pos(shape, seq, n_ctx, row0):
    t = row0 + lax.broadcasted_iota(jnp.int32, shape, 0)
    is_lat = t < seq
    return jnp.where(is_lat, t, t - seq), jnp.where(is_lat, seq, n_ctx)


def _shift_rows(u, k, bounds=None):
    rolled = pltpu.roll(u, (-k) % u.shape[0], axis=0)
    if bounds is None:
        return rolled
    pos, seg_len = bounds
    valid = (pos >= -k) if k < 0 else (pos < seg_len - k)
    return jnp.where(valid, rolled, 0.0)


def _edge_windows(seq, t_all):
    p, h = EDGE_ROWS, EDGE_HALO
    assert seq >= p + h and t_all - seq >= 2 * (p + h)
    return ((0, p + h, 0, p), (seq - p - h, 2 * (p + h), h, 2 * p), (t_all - p - h, p + h, h, p))


def _redo_edges(fn, u_ref, dst_ref, seq, t_all):
    for row0, rows, off, n in _edge_windows(seq, t_all):
        uw = u_ref[row0:row0 + rows, :].astype(_F32)
        res = fn(uw, _segment_pos(uw.shape, seq, t_all - seq, row0))
        dst_ref[row0 + off:row0 + off + n, :] = res[off:off + n].astype(dst_ref.dtype)


def _mlstm_pre_kernel(u_ref, cw_ref, cb_ref, wq_ref, wk_ref, wv_ref, wg_ref, gb_ref,
                      xc_ref, q_ref, k_ref, v_ref, g_ref, acc_sc, *, seq, n_ctx, k_scale, n_gate):
    j = pl.program_id(1)
    ub = u_ref[...]
    left = (CONV_W - 1) // 2

    def conv(u, bounds):
        acc = cb_ref[...] + u * cw_ref[left:left + 1, :]
        for jj in range(CONV_W):
            if jj != left:
                acc = acc + _shift_rows(u, jj - left, bounds) * cw_ref[jj:jj + 1, :]
        return acc

    acc_sc[...] = conv(ub.astype(_F32), None)
    _redo_edges(conv, u_ref, acc_sc, seq, seq + n_ctx)
    xc = _silu(acc_sc[...])
    xcb = xc.astype(_BF16)
    te = ub.shape[1]
    own_block = (lax.broadcasted_iota(jnp.int32, (te, te), 0) // QKV_BLOCK
                 == lax.broadcasted_iota(jnp.int32, (te, te), 1) // QKV_BLOCK)
    dense = lambda w_ref: jnp.where(own_block, w_ref[...].astype(_F32), 0.0).astype(_BF16)
    q = jnp.dot(xcb, dense(wq_ref), preferred_element_type=_F32)
    k = jnp.dot(xcb, dense(wk_ref), preferred_element_type=_F32)
    v = jnp.dot(ub, dense(wv_ref), preferred_element_type=_F32)
    qb, kb, vb = q.astype(_BF16), k.astype(_BF16), v.astype(_BF16)
    xc_ref[...] = xcb
    q_ref[...] = qb
    k_ref[...] = (k * k_scale).astype(_BF16)
    v_ref[...] = vb
    part = jnp.dot(jnp.concatenate([qb, kb, vb], axis=1), wg_ref[...], preferred_element_type=_F32)

    @pl.when(j == 0)
    def _():
        g_ref[...] = part + gb_ref[...]

    @pl.when(j > 0)
    def _():
        g_ref[...] += part

    @pl.when(j == pl.num_programs(1) - 1)
    def _():
        g = g_ref[...]
        col = lax.broadcasted_iota(jnp.int32, g.shape, 1)
        log_sig = jnp.minimum(g, 0.0) - jnp.log1p(jnp.exp(-jnp.abs(g)))
        g_ref[...] = jnp.where(col < n_gate, g, log_sig)


def _blockdiag_rows(w, te):
    rows = w.reshape(w.shape[0] * QKV_BLOCK, QKV_BLOCK).astype(_BF16)
    return jnp.tile(rows, (1, te // QKV_BLOCK))


def _mlstm_pre(uz, conv_w, conv_b, wq, wk, wv, w_ig, b_ig, w_fg, b_fg, seq):
    bsz, t_all, e2 = uz.shape
    e = e2 // 2
    nh = w_ig.shape[-1]
    te = _tile(e, 256, LANES)
    nt = e // te
    ng = 4 * nh
    wg = jnp.concatenate([w_ig[0], w_ig[1], w_fg[0], w_fg[1]], axis=-1)
    wg = wg.reshape(3, nt, te, ng).transpose(1, 0, 2, 3).reshape(nt, 3 * te, ng).astype(_BF16)
    gb = jnp.concatenate([b_ig[0], b_ig[1], b_fg[0], b_fg[1]], axis=-1).reshape(1, ng)
    tok = pl.BlockSpec((None, t_all, te), lambda b, j: (b, 0, j))
    bd = pl.BlockSpec((te, te), lambda b, j: (j, 0))
    act = jax.ShapeDtypeStruct((bsz, t_all, e), _BF16)
    return pl.pallas_call(
        functools.partial(_mlstm_pre_kernel, seq=seq, n_ctx=t_all - seq,
                          k_scale=float((e // nh) ** -0.5), n_gate=2 * nh),
        grid=(bsz, nt),
        in_specs=[
            tok,
            pl.BlockSpec((CONV_W, te), lambda b, j: (0, j)),
            pl.BlockSpec((1, te), lambda b, j: (0, j)),
            bd, bd, bd,
            pl.BlockSpec((None, 3 * te, ng), lambda b, j: (j, 0, 0)),
            pl.BlockSpec((1, ng), lambda b, j: (0, 0)),
        ],
        out_specs=[tok, tok, tok, tok, pl.BlockSpec((None, t_all, ng), lambda b, j: (b, 0, 0))],
        out_shape=[act, act, act, act, jax.ShapeDtypeStruct((bsz, t_all, ng), _F32)],
        scratch_shapes=[pltpu.VMEM((t_all, te), _F32)],
        compiler_params=_cparams("parallel", "arbitrary"),
        name="mlstm_pre",
    )(uz, conv_w, conv_b.reshape(1, e), _blockdiag_rows(wq, te), _blockdiag_rows(wk, te),
      _blockdiag_rows(wv, te), wg, gb)


def _scan_kernel(q_ref, k_ref, v_ref, li_ref, lf_ref, *rest, rev):
    if rev:
        hf_ref, xc_ref, z_ref, nw_ref, sk_ref, o_ref, c_sc, cb_sc, n_sc, m_sc = rest
    else:
        o_ref, c_sc, cb_sc, n_sc, m_sc = rest

    @pl.when(pl.program_id(2) == 0)
    def _():
        c_sc[...] = jnp.zeros(c_sc.shape, _F32)
        cb_sc[...] = jnp.zeros(cb_sc.shape, _BF16)
        n_sc[...] = jnp.zeros(n_sc.shape, _F32)
        m_sc[...] = jnp.zeros(m_sc.shape, _F32)

    ln = q_ref.shape[0]
    dh = c_sc.shape[-1]
    ti = lax.broadcasted_iota(jnp.int32, (ln, ln), 0)
    si = lax.broadcasted_iota(jnp.int32, (ln, ln), 1)
    incl = si >= ti if rev else si <= ti
    incl_t = ti >= si if rev else ti <= si
    eye = si == ti
    for hh in range(c_sc.shape[0]):
        cols = slice(hh * dh, (hh + 1) * dh)
        h = _scan_chunk(q_ref[:, cols], k_ref[:, cols], v_ref[:, cols], li_ref[hh], lf_ref[hh],
                        incl, incl_t, eye, c_sc.at[hh], cb_sc.at[hh], n_sc.at[hh], m_sc.at[hh])
        if rev:
            h = h + hf_ref[:, cols].astype(_F32)
            mu = jnp.mean(h, axis=-1, keepdims=True)
            hc = h - mu
            var = jnp.mean(hc * hc, axis=-1, keepdims=True)
            hn = hc * lax.rsqrt(var + EPS) * nw_ref[:, cols]
            h = (hn + sk_ref[:, cols] * xc_ref[:, cols].astype(_F32)) * _silu(z_ref[:, cols].astype(_F32))
        o_ref[:, cols] = h.astype(o_ref.dtype)


def _scan_chunk(q, k, v, li_r, lf_r, incl, incl_t, eye, c_sc, cb_sc, n_sc, m_sc):
    ln = q.shape[0]
    lf_b = jnp.broadcast_to(lf_r, (ln, ln))
    li_b = jnp.broadcast_to(li_r, (ln, ln))
    b_col = jnp.sum(jnp.where(incl, lf_b, 0.0), axis=1, keepdims=True)
    lf_col = jnp.sum(jnp.where(eye, lf_b, 0.0), axis=1, keepdims=True)
    li_col = jnp.sum(jnp.where(eye, li_b, 0.0), axis=1, keepdims=True)
    b_row = jnp.sum(jnp.where(incl_t, lf_col, 0.0), axis=0, keepdims=True)
    b_end = jnp.sum(lf_r, axis=1, keepdims=True)
    m_old = m_sc[...]
    g_row = b_end - b_row + li_r
    g_col = b_end - b_col + li_col
    m_new = jnp.maximum(b_end + m_old, jnp.max(g_row, axis=1, keepdims=True))
    wg_col = jnp.exp(g_col - m_new)
    decay = jnp.exp(b_end + m_old - m_new)

    logw = jnp.where(incl, b_col - b_row + li_r, -jnp.inf)
    inter = b_col + m_old
    m_t = jnp.maximum(jnp.max(logw, axis=1, keepdims=True), inter)
    s = lax.dot_general(q, k, (((1,), (1,)), ((), ())), preferred_element_type=_F32) * jnp.exp(logw - m_t)
    w_inter = jnp.exp(inter - m_t)
    den = (jnp.sum(s, axis=1, keepdims=True)
           + w_inter * jnp.sum(q.astype(_F32) * n_sc[...], axis=1, keepdims=True))
    inv = 1.0 / jnp.maximum(jnp.abs(den), jnp.exp(-m_t))
    sb = s.astype(_BF16)
    kw = k.astype(_F32) * wg_col
    kwb = kw.astype(_BF16)
    n_sc[...] = decay * n_sc[...] + jnp.sum(kw, axis=0, keepdims=True)
    m_sc[...] = m_new
    dv = v.shape[1]
    step = min(dv, 2 * LANES)
    pieces = []
    for c0 in range(0, dv, step):
        cols = slice(c0, c0 + step)
        vc = v[:, cols]
        num = (jnp.dot(sb, vc, preferred_element_type=_F32)
               + w_inter * jnp.dot(q, cb_sc[:, cols], preferred_element_type=_F32))
        pieces.append(num * inv)
        c_new = decay * c_sc[:, cols] + lax.dot_general(kwb, vc, (((0,), (0,)), ((), ())),
                                                        preferred_element_type=_F32)
        c_sc[:, cols] = c_new
        cb_sc[:, cols] = c_new.astype(_BF16)
    return jnp.concatenate(pieces, axis=1)


def _mlstm_scan(q, k, v, gates, uz, xc, hnorm_w, skip, seq, nh):
    bsz, t_all, e = q.shape
    dh = e // nh
    n_ctx = t_all - seq
    ln = _tile(math.gcd(seq, n_ctx), SCAN_CHUNK, 8)
    nc = t_all // ln
    ncl = seq // ln
    ncc = n_ctx // ln
    g = gates.reshape(bsz, nc, ln, 2, 2, nh).transpose(3, 4, 0, 5, 1, 2).reshape(2, 2, bsz, nh, nc, 1, ln)
    act = jax.ShapeDtypeStruct((bsz, t_all, e), _BF16)

    def run(rev, extra_in, extra_specs):
        hb = SCAN_HEADS_PER_STEP[int(rev)]
        hb = hb if nh % hb == 0 else 1
        scratch = [pltpu.VMEM((hb, dh, dh), _F32), pltpu.VMEM((hb, dh, dh), _BF16),
                   pltpu.VMEM((hb, 1, dh), _F32), pltpu.VMEM((hb, 1, 1), _F32)]

        def chunk(s):
            if rev:
                return jnp.where(s < ncc, ncl + ncc - 1 - s, ncl - 1 - (s - ncc))
            return jnp.where(s < ncc, ncl + s, s - ncc)

        tok = lambda col0: pl.BlockSpec((None, ln, hb * dh), lambda b, hh, s: (b, chunk(s), col0 + hh))
        gate = pl.BlockSpec((None, hb, None, 1, ln), lambda b, hh, s: (b, hh, chunk(s), 0, 0))
        return pl.pallas_call(
            functools.partial(_scan_kernel, rev=rev),
            grid=(bsz, nh // hb, nc),
            in_specs=[tok(0), tok(0), tok(0), gate, gate, *extra_specs(tok, hb)],
            out_specs=tok(0),
            out_shape=act,
            scratch_shapes=scratch,
            compiler_params=_cparams("parallel", "parallel", "arbitrary"),
            name="mlstm_scan_rev" if rev else "mlstm_scan_fwd",
        )(q, k, v, g[0, int(rev)], g[1, int(rev)], *extra_in)

    h_fwd = run(False, (), lambda tok, hb: ())
    vec = lambda hb: pl.BlockSpec((1, hb * dh), lambda b, hh, s: (0, hh))
    return run(True, (h_fwd, xc, uz, hnorm_w.reshape(1, e), skip.reshape(1, e)),
               lambda tok, hb: (tok(0), tok(0), tok(nh // hb), vec(hb), vec(hb)))


def _mlstm_branch(uz, conv_w, conv_b, wq, wk, wv, w_ig, b_ig, w_fg, b_fg, hnorm_w, skip, seq):
    nh = w_ig.shape[-1]
    xc, q, k, v, gates = _mlstm_pre(uz, conv_w, conv_b, wq, wk, wv, w_ig, b_ig, w_fg, b_fg, seq)
    return _mlstm_scan(q, k, v, gates, uz, xc, hnorm_w, skip, seq, nh)


def _pool_kernel(u_ref, o_ref, *, seq, n_ctx, tiles_per_group):
    grp = pl.program_id(1) // tiles_per_group
    for gi, w in enumerate(POOL_WINDOWS):
        lo = w // 2
        hi = w - 1 - lo
        assert hi == lo - 1 and lo <= EDGE_ROWS

        def delta(u, bounds, w=w, lo=lo, hi=hi):
            trail = u
            lead = u
            span = 1
            while span < lo:
                trail = trail + _shift_rows(trail, -span, bounds)
                lead = lead + _shift_rows(lead, span, bounds)
                span *= 2
            win = _shift_rows(trail, -1, bounds) + lead
            if bounds is None:
                return win * (1.0 / w) - u
            pos, seg_len = bounds
            cnt = jnp.minimum(pos + hi + 1, seg_len) - jnp.maximum(pos - lo, 0)
            return win / cnt.astype(_F32) - u

        @pl.when(grp == gi)
        def _(delta=delta):
            o_ref[...] = delta(u_ref[...].astype(_F32), None).astype(o_ref.dtype)
            _redo_edges(delta, u_ref, o_ref, seq, seq + n_ctx)


def _pool_group_linear(a, w_grp, uz, scale):
    bsz, t_all, e = a.shape
    ng, gw, _ = w_grp.shape
    tm = _tile(t_all, 1088, 16)
    tn = _tile(gw, 1024, LANES)
    npg = gw // tn

    def epilogue(acc, pids, z_ref, sc_ref):
        return acc * sc_ref[...] * _silu(z_ref[...].astype(_F32))

    extras = [uz, scale.reshape(1, e)]
    especs = [pl.BlockSpec((None, tm, tn), lambda b, g, j, i, k: (b, i, (e // tn) + g * npg + j)),
              pl.BlockSpec((1, tn), lambda b, g, j, i, k: (0, g * npg + j))]
    return _matmul(
        (bsz, ng, npg, t_all // tm, 1),
        a, pl.BlockSpec((None, tm, gw), lambda b, g, j, i, k: (b, i, g)),
        w_grp, pl.BlockSpec((None, gw, tn), lambda b, g, j, i, k: (g, 0, j)),
        jax.ShapeDtypeStruct((bsz, t_all, e), _BF16),
        pl.BlockSpec((None, tm, tn), lambda b, g, j, i, k: (b, i, g * npg + j)),
        extras=tuple(extras), extra_specs=tuple(especs), epilogue=epilogue, name="pool_group")


def _pool_branch(uz, w_grp, scale, seq):
    bsz, t_all, e2 = uz.shape
    e = e2 // 2
    te = _tile(e // N_GROUPS, 256, LANES)
    tok = pl.BlockSpec((None, t_all, te), lambda b, j: (b, 0, j))
    dlt = pl.pallas_call(
        functools.partial(_pool_kernel, seq=seq, n_ctx=t_all - seq, tiles_per_group=e // N_GROUPS // te),
        grid=(bsz, e // te),
        in_specs=[tok],
        out_specs=tok,
        out_shape=jax.ShapeDtypeStruct((bsz, t_all, e), _BF16),
        compiler_params=_cparams("parallel", "parallel"),
        name="pool",
    )(uz)
    return _pool_group_linear(dlt, w_grp, uz, scale)


def _cos_sin(n_out, n_in, period):
    ang = (np.outer(np.arange(n_out), np.arange(n_in)) % period) * (2.0 * np.pi / period)
    return np.cos(ang), np.sin(ang)


@functools.lru_cache(maxsize=None)
def _chan_table(gw):
    c, s = _cos_sin(gw, gw, gw)
    return np.concatenate([c, s], axis=0) * gw ** -0.5


@functools.lru_cache(maxsize=None)
def _time_tables(seq, n_ctx):
    n1 = max(f for f in range(1, math.isqrt(seq) + 1) if seq % f == 0)
    n2 = seq // n1
    g = SUBLANES
    assert n1 % g == 0 and n2 % g == 0
    eye = np.eye(g)
    c1, s1 = _cos_sin(n1, n1, n1)
    base = np.block([[c1, -s1], [-s1, -c1]]) * seq ** -0.5
    stage1 = np.einsum('ab,rkst->arkstb', eye, base.reshape(2, n1, 2, n1)).reshape(g * 2 * n1, 2 * n1 * g)
    twc, tws = _cos_sin(n2, n1, seq)
    c2, s2 = _cos_sin(n2, n2, n2)
    cs2 = np.concatenate([c2, s2], axis=1)
    stage2 = np.einsum('ab,krt->kartb', eye, cs2.reshape(n2, 2, n2)).reshape(n2 * g, 2 * n2 * g)
    cc, sc = _cos_sin(n_ctx, n_ctx, n_ctx)
    ctx_tab = np.concatenate([cc, -sc], axis=1) * n_ctx ** -0.5
    lanes = lambda a: np.repeat(a[:, :, None], LANES, axis=2)
    return n1, n2, stage1, lanes(twc), lanes(tws), stage2, ctx_tab


def _time_dft_kernel(y1_ref, y2_ref, z_ref, w1_ref, twc_ref, tws_ref, w2_ref, wc_ref, o_ref,
                     y1f, y2f, ar, ai, out, *, seq, n1, n2):
    g = SUBLANES
    tc = o_ref.shape[-1]
    wide = lambda tab: jnp.concatenate([tab] * (tc // LANES), axis=1)
    y1f[...] = y1_ref[:seq, :].astype(_F32).reshape(y1f.shape)
    y2f[...] = y2_ref[:seq, :].astype(_F32).reshape(y2f.shape)
    for th in range(n2 // g):
        rhs = jnp.concatenate([y1f[:, th].reshape(n1 * g, tc), y2f[:, th].reshape(n1 * g, tc)], axis=0)
        res = jnp.dot(w1_ref[...], rhs.astype(_BF16), preferred_element_type=_F32)
        for tl in range(g):
            t2 = th * g + tl
            a_r = res[tl * 2 * n1:tl * 2 * n1 + n1]
            a_i = res[tl * 2 * n1 + n1:(tl + 1) * 2 * n1]
            c, s = wide(twc_ref[t2]), wide(tws_ref[t2])
            ar[t2] = (a_r * c + a_i * s).reshape(ar.shape[1:])
            ai[t2] = (a_i * c - a_r * s).reshape(ai.shape[1:])
    for kh in range(n1 // g):
        rhs = jnp.concatenate([ar[:, kh].reshape(n2 * g, tc), ai[:, kh].reshape(n2 * g, tc)], axis=0)
        res = jnp.dot(w2_ref[...], rhs.astype(_BF16), preferred_element_type=_F32)
        out[:, kh] = res.reshape(n2, g, tc)
    gate = _silu(z_ref[:seq, :].astype(_F32))
    o_ref[:seq, :] = (out[...].reshape(seq, tc) * gate).astype(o_ref.dtype)
    rhs = jnp.concatenate([y1_ref[seq:, :], y2_ref[seq:, :]], axis=0)
    res = jnp.dot(wc_ref[...], rhs, preferred_element_type=_F32)
    o_ref[seq:, :] = (res * _silu(z_ref[seq:, :].astype(_F32))).astype(o_ref.dtype)


def _fourier_branch(uz, w_grp, seq):
    bsz, t_all, e2 = uz.shape
    n_ctx = t_all - seq
    e = e2 // 2
    gw = e // N_GROUPS
    tab = jnp.asarray(_chan_table(gw), dtype=_BF16)
    tw = _tile(gw, 1024, LANES)
    nrow = gw // tw
    wcomb = _matmul(
        (N_GROUPS, gw // tw, 2 * nrow, 1),
        tab, pl.BlockSpec((tw, gw), lambda g, j, i, k: (i, 0)),
        w_grp, pl.BlockSpec((None, gw, tw), lambda g, j, i, k: (g, 0, j)),
        jax.ShapeDtypeStruct((N_GROUPS, gw, 2 * gw), _BF16),
        pl.BlockSpec((None, tw, tw), lambda g, j, i, k: (g, i % nrow, (i // nrow) * (gw // tw) + j)),
        name="fourier_weight")
    tm = _tile(t_all, 1088, 16)
    tn = _tile(gw, 1024, LANES)
    y = _matmul(
        (bsz, N_GROUPS, 2 * gw // tn, t_all // tm, 1),
        uz, pl.BlockSpec((None, tm, gw), lambda b, g, j, i, k: (b, i, g)),
        wcomb, pl.BlockSpec((None, gw, tn), lambda b, g, j, i, k: (g, 0, j)),
        jax.ShapeDtypeStruct((bsz, t_all, 2 * e), _BF16),
        pl.BlockSpec((None, tm, tn), lambda b, g, j, i, k: (b, i, g * (2 * gw // tn) + j)),
        name="dft_chan")
    n1, n2, stage1, twc, tws, stage2, ctx_tab = _time_tables(seq, n_ctx)
    tc = _tile(gw, 256, LANES)
    cpg = gw // tc

    def ycol(part):
        return lambda b, c: (b, 0, (2 * (c // cpg) + part) * cpg + c % cpg)

    tok = lambda imap: pl.BlockSpec((None, t_all, tc), imap)
    full = lambda a: pl.BlockSpec(a.shape, lambda b, c: (0,) * a.ndim, pipeline_mode=pl.Buffered(1))
    tabs = [jnp.asarray(stage1, dtype=_BF16), jnp.asarray(twc, dtype=_F32), jnp.asarray(tws, dtype=_F32),
            jnp.asarray(stage2, dtype=_BF16), jnp.asarray(ctx_tab, dtype=_BF16)]
    by_t1 = pltpu.VMEM((n1, n2 // SUBLANES, SUBLANES, tc), _F32)
    by_t2 = pltpu.VMEM((n2, n1 // SUBLANES, SUBLANES, tc), _F32)
    return pl.pallas_call(
        functools.partial(_time_dft_kernel, seq=seq, n1=n1, n2=n2),
        grid=(bsz, e // tc),
        in_specs=[tok(ycol(0)), tok(ycol(1)), tok(lambda b, c: (b, 0, e // tc + c))] + [full(t) for t in tabs],
        out_specs=tok(lambda b, c: (b, 0, c)),
        out_shape=jax.ShapeDtypeStruct((bsz, t_all, e), _BF16),
        scratch_shapes=[by_t1, by_t1, by_t2, by_t2, by_t2],
        compiler_params=_cparams("parallel", "parallel"),
        name="dft_time",
    )(y, y, uz, *tabs)


def kernel(x, c, ctx, c_ctx, ada_w, ada_b, norm_g, final_g, a_w_in, a_conv_w, a_conv_b, a_wq, a_wk, a_wv,
           a_w_ig, a_b_ig, a_w_fg, a_b_fg, a_hnorm_w, a_skip, a_w_out, b_w_in, b_w_grp, b_scale, b_w_out,
           c_w_in, c_w_grp, c_w_out):
    bsz, seq, d = x.shape
    depth = ada_w.shape[0]
    n_rows = 8
    assert bsz + 1 <= n_rows
    rows = jnp.concatenate([c, c_ctx[None], jnp.zeros((n_rows - bsz - 1, d), c.dtype)], axis=0)
    mods = _ada(rows, ada_w, ada_b).reshape(depth, n_rows, 1, 3 * d)
    xs = _embed(x, ctx)
    for i in range(depth):
        kind = i % N_MIXERS
        j = i // N_MIXERS
        h = _prenorm(xs, norm_g[i], mods[i], seq)
        w_in, w_out = ((a_w_in, a_w_out), (b_w_in, b_w_out), (c_w_in, c_w_out))[kind]
        uz = _in_proj(h, w_in, j)
        if kind == 0:
            a = _mlstm_branch(uz, a_conv_w[j], a_conv_b[j], a_wq[j], a_wk[j], a_wv[j],
                              a_w_ig[j], a_b_ig[j], a_w_fg[j], a_b_fg[j], a_hnorm_w[j], a_skip[j], seq)
        elif kind == 1:
            a = _pool_branch(uz, b_w_grp[j], b_scale[j], seq)
        else:
            a = _fourier_branch(uz, c_w_grp[j], seq)
        xs = _out_proj(a, w_out, j, xs, mods[i], seq)
    return _final_norm(xs, final_g, seq)
```

```python
import functools
import math

import jax
import jax.numpy as jnp
import numpy as np
from jax import lax
from jax.experimental import pallas as pl
from jax.experimental.pallas import tpu as pltpu

GRID_W = 64
N_MIXERS = 3
QKV_BLOCK = 4
CONV_W = 4
POOL_WINDOWS = (2, 4, 8, 16)
N_GROUPS = 4
POS_BASE = 10000.0
EPS = 1e-6

LANES = 128
SUBLANES = 8
VMEM_LIMIT_BYTES = 56 * 1024 * 1024
SCAN_CHUNK = 256
SCAN_HEADS_PER_STEP = (4, 2)

_F32 = jnp.float32
_BF16 = jnp.bfloat16


def _cparams(*sem):
    return pltpu.CompilerParams(dimension_semantics=sem, vmem_limit_bytes=VMEM_LIMIT_BYTES)


def _tile(n, target, align):
    best = None
    for t in range(align, min(n, target) + 1, align):
        if n % t == 0:
            best = t
    return n if best is None else best


def _silu(v):
    half = 0.5 * v
    return half + half * jnp.tanh(half)


def _matmul(grid, a, a_spec, b, b_spec, out_shape, out_spec, extras=(), extra_specs=(), epilogue=None,
            name="matmul"):
    n_extra = len(extras)
    if epilogue is None:
        epilogue = lambda acc, pids: acc

    def body(*refs):
        a_ref, b_ref = refs[0], refs[1]
        ex = refs[2:2 + n_extra]
        o_ref = refs[2 + n_extra]
        pids = tuple(pl.program_id(ax) for ax in range(len(grid)))
        acc = jnp.dot(a_ref[...].astype(_BF16), b_ref[...].astype(_BF16), preferred_element_type=_F32)
        o_ref[...] = epilogue(acc, pids, *ex).astype(o_ref.dtype)

    return pl.pallas_call(
        body,
        grid=grid,
        in_specs=[a_spec, b_spec, *extra_specs],
        out_specs=out_spec,
        out_shape=out_shape,
        compiler_params=_cparams(*(("parallel",) * len(grid))),
        name=name,
    )(a, b, *extras)


def _proj(a, w, layer, tm, tn, out_dtype, name, extras=(), extra_blocks=(), epilogue=None):
    bsz, t_all, k = a.shape
    n = w.shape[2]
    nt, mt = n // tn, t_all // tm
    pieces = bsz * mt
    tkp = k // pieces
    assert tkp * pieces == k and tkp % 16 == 0
    n_extra = len(extras)
    if epilogue is None:
        epilogue = lambda acc, pids: acc

    def body(*refs):
        a_ref, w_ref = refs[0], refs[1]
        ex = refs[2:2 + n_extra]
        o_ref, w_bf = refs[2 + n_extra], refs[3 + n_extra]
        pids = tuple(pl.program_id(ax) for ax in range(3))
        jj = pids[0]
        piece = pids[1] * mt + pids[2]

        def stage():
            rows = pl.ds(pl.multiple_of(piece * tkp, tkp), tkp)
            w_bf[jj % 2, rows, :] = w_ref[...].astype(_BF16)

        @pl.when(jj == 0)
        def _():
            stage()

        @pl.when(jj > 0)
        def _():
            stage()
            acc = jnp.dot(a_ref[...], w_bf[(jj + 1) % 2], preferred_element_type=_F32)
            o_ref[...] = epilogue(acc, pids, *ex).astype(o_ref.dtype)

    def a_map(jj, b, i):
        return (jnp.where(jj > 0, b, 0), jnp.where(jj > 0, i, 0), 0)

    def o_map(jj, b, i):
        return (jnp.where(jj > 0, b, 0), jnp.where(jj > 0, i, 0), jnp.maximum(jj - 1, 0))

    def w_map(jj, b, i):
        last = jj == nt
        return (layer, jnp.where(last, pieces - 1, b * mt + i), jnp.minimum(jj, nt - 1))

    return pl.pallas_call(
        body,
        grid=(nt + 1, bsz, mt),
        in_specs=[pl.BlockSpec((None, tm, k), a_map),
                  pl.BlockSpec((None, tkp, tn), w_map),
                  *[spec(o_map) for spec in extra_blocks]],
        out_specs=pl.BlockSpec((None, tm, tn), o_map),
        out_shape=jax.ShapeDtypeStruct((bsz, t_all, n), out_dtype),
        scratch_shapes=[pltpu.VMEM((2, k, tn), _BF16)],
        compiler_params=_cparams("arbitrary", "arbitrary", "arbitrary"),
        name=name,
    )(a, w, *extras)


def _ada_kernel(s_ref, w_ref, b_ref, o_ref):
    s = _silu(s_ref[...]).astype(_BF16)
    acc = jnp.dot(s, w_ref[...].astype(_BF16), preferred_element_type=_F32)
    o_ref[...] = acc + b_ref[...]


def _ada(rows, ada_w, ada_b):
    depth, d, n3 = ada_w.shape
    nr = rows.shape[0]
    tn = _tile(n3, 1024, LANES)
    return pl.pallas_call(
        _ada_kernel,
        grid=(depth, n3 // tn),
        in_specs=[
            pl.BlockSpec((nr, d), lambda i, n: (0, 0)),
            pl.BlockSpec((None, d, tn), lambda i, n: (i, 0, n)),
            pl.BlockSpec((None, 1, tn), lambda i, n: (i, 0, n)),
        ],
        out_specs=pl.BlockSpec((None, nr, tn), lambda i, n: (i, 0, n)),
        out_shape=jax.ShapeDtypeStruct((depth, nr, n3), _F32),
        compiler_params=_cparams("parallel", "parallel"),
        name="ada",
    )(rows, ada_w, ada_b.reshape(depth, 1, n3))


def _embed_kernel(x_ref, c_ref, re_ref, ce_ref, o_ref, *, n_lat, half):
    r = pl.program_id(1)

    @pl.when(r < n_lat)
    def _():
        for i in range(re_ref.shape[0]):
            rows = slice(i * GRID_W, (i + 1) * GRID_W)
            o_ref[rows, :half] = x_ref[rows, :half] + re_ref[i:i + 1, :]
            o_ref[rows, half:] = x_ref[rows, half:] + ce_ref[...]

    @pl.when(r >= n_lat)
    def _():
        o_ref[...] = c_ref[...]


def _embed(x, ctx):
    bsz, seq, d = x.shape
    n_ctx = ctx.shape[1]
    n_rows = seq // GRID_W
    rb = max(f for f in (4, 2, 1) if n_rows % f == 0 and n_ctx % (f * GRID_W) == 0)
    tb = rb * GRID_W
    n_lat = seq // tb
    half = d // 2
    quarter = d // 4
    omega = 1.0 / (POS_BASE ** (jnp.arange(quarter, dtype=_F32) / quarter))

    def axis_emb(p):
        ang = p[:, None] * omega[None, :]
        return jnp.concatenate([jnp.sin(ang), jnp.cos(ang)], axis=-1)

    row_emb = axis_emb(jnp.arange(n_rows, dtype=_F32)).reshape(n_lat, rb, half)
    col_emb = axis_emb(jnp.arange(GRID_W, dtype=_F32))
    return pl.pallas_call(
        functools.partial(_embed_kernel, n_lat=n_lat, half=half),
        grid=(bsz, n_lat + n_ctx // tb),
        in_specs=[
            pl.BlockSpec((None, tb, d), lambda b, r: (b, jnp.minimum(r, n_lat - 1), 0)),
            pl.BlockSpec((None, tb, d), lambda b, r: (b, jnp.maximum(r - n_lat, 0), 0)),
            pl.BlockSpec((None, rb, half), lambda b, r: (jnp.minimum(r, n_lat - 1), 0, 0)),
            pl.BlockSpec((GRID_W, half), lambda b, r: (0, 0)),
        ],
        out_specs=pl.BlockSpec((None, tb, d), lambda b, r: (b, r, 0)),
        out_shape=jax.ShapeDtypeStruct((bsz, seq + n_ctx, d), x.dtype),
        compiler_params=_cparams("parallel", "parallel"),
        name="embed",
    )(x, ctx, row_emb, col_emb)


def _prenorm_kernel(x_ref, g_ref, sh_ref, sc_ref, o_ref):
    x = x_ref[...]
    y = x * lax.rsqrt(jnp.mean(x * x, axis=-1, keepdims=True) + EPS)
    o_ref[...] = (y * g_ref[...] * (1.0 + sc_ref[...]) + sh_ref[...]).astype(o_ref.dtype)


def _prenorm(xs, g, mods, seq):
    bsz, t_all, d = xs.shape
    tb = _tile(math.gcd(seq, t_all - seq), 256, 8)
    n_lat = seq // tb

    def mrow(b, j):
        return jnp.where(j < n_lat, b, bsz)

    return pl.pallas_call(
        _prenorm_kernel,
        grid=(bsz, t_all // tb),
        in_specs=[
            pl.BlockSpec((None, tb, d), lambda b, j: (b, j, 0)),
            pl.BlockSpec((1, d), lambda b, j: (0, 0)),
            pl.BlockSpec((None, 1, d), lambda b, j: (mrow(b, j), 0, 0)),
            pl.BlockSpec((None, 1, d), lambda b, j: (mrow(b, j), 0, 1)),
        ],
        out_specs=pl.BlockSpec((None, tb, d), lambda b, j: (b, j, 0)),
        out_shape=jax.ShapeDtypeStruct((bsz, t_all, d), _BF16),
        compiler_params=_cparams("parallel", "parallel"),
        name="prenorm",
    )(xs, g.reshape(1, d), mods, mods)


def _final_norm_kernel(x_ref, g_ref, o_ref):
    x = x_ref[...]
    y = x * lax.rsqrt(jnp.mean(x * x, axis=-1, keepdims=True) + EPS)
    o_ref[...] = y * g_ref[...]


def _final_norm(xs, g, seq):
    bsz, _, d = xs.shape
    tb = _tile(seq, 512, 8)
    return pl.pallas_call(
        _final_norm_kernel,
        grid=(bsz, seq // tb),
        in_specs=[
            pl.BlockSpec((None, tb, d), lambda b, j: (b, j, 0)),
            pl.BlockSpec((1, d), lambda b, j: (0, 0)),
        ],
        out_specs=pl.BlockSpec((None, tb, d), lambda b, j: (b, j, 0)),
        out_shape=jax.ShapeDtypeStruct((bsz, seq, d), xs.dtype),
        compiler_params=_cparams("parallel", "parallel"),
        name="final_norm",
    )(xs, g.reshape(1, d))


def _in_proj(h, w, layer):
    t_all, n = h.shape[1], w.shape[2]
    return _proj(h, w, layer, _tile(t_all, 1088, 16), _tile(n, 1024, LANES), _BF16, "in_proj")


def _out_proj(a, w, layer, xs, mods, seq):
    bsz, t_all, _ = a.shape
    d = w.shape[2]
    tm = _tile(t_all, 544, 16)
    tn = _tile(d, 512, LANES)
    g_off = 2 * (d // tn)

    def epilogue(acc, pids, res_ref, gl_ref, gc_ref):
        row = pids[2] * tm + lax.broadcasted_iota(jnp.int32, acc.shape, 0)
        gate = jnp.where(row < seq, gl_ref[...], gc_ref[...])
        return res_ref[...] + gate * acc

    def gate_block(row_of):
        return lambda om: pl.BlockSpec((None, 1, tn), lambda jj, b, i: (row_of(b), 0, g_off + om(jj, b, i)[2]))

    return _proj(a, w, layer, tm, tn, xs.dtype, "out_proj", extras=(xs, mods, mods),
                 extra_blocks=(lambda om: pl.BlockSpec((None, tm, tn), om),
                               gate_block(lambda b: b), gate_block(lambda b: bsz)),
                 epilogue=epilogue)


EDGE_ROWS = 16
EDGE_HALO = 16


def _segment_pos(shape, seq, n_ctx, row0):
    t = row0 + lax.broadcasted_iota(jnp.int32, shape, 0)
    is_lat = t < seq
    return jnp.where(is_lat, t, t - seq), jnp.where(is_lat, seq, n_ctx)


def _shift_rows(u, k, bounds=None):
    rolled = pltpu.roll(u, (-k) % u.shape[0], axis=0)
    if bounds is None:
        return rolled
    pos, seg_len = bounds
    valid = (pos >= -k) if k < 0 else (pos < seg_len - k)
    return jnp.where(valid, rolled, 0.0)


def _edge_windows(seq, t_all):
    p, h = EDGE_ROWS, EDGE_HALO
    assert seq >= p + h and t_all - seq >= 2 * (p + h)
    return ((0, p + h, 0, p), (seq - p - h, 2 * (p + h), h, 2 * p), (t_all - p - h, p + h, h, p))


def _redo_edges(fn, u_ref, dst_ref, seq, t_all):
    for row0, rows, off, n in _edge_windows(seq, t_all):
        uw = u_ref[row0:row0 + rows, :].astype(_F32)
        res = fn(uw, _segment_pos(uw.shape, seq, t_all - seq, row0))
        dst_ref[row0 + off:row0 + off + n, :] = res[off:off + n].astype(dst_ref.dtype)


def _mlstm_pre_kernel(u_ref, cw_ref, cb_ref, wq_ref, wk_ref, wv_ref, wg_ref, gb_ref,
                      xc_ref, q_ref, k_ref, v_ref, g_ref, acc_sc, *, seq, n_ctx, k_scale, n_gate):
    j = pl.program_id(1)
    ub = u_ref[...]
    left = (CONV_W - 1) // 2

    def conv(u, bounds):
        acc = cb_ref[...] + u * cw_ref[left:left + 1, :]
        for jj in range(CONV_W):
            if jj != left:
                acc = acc + _shift_rows(u, jj - left, bounds) * cw_ref[jj:jj + 1, :]
        return acc

    acc_sc[...] = conv(ub.astype(_F32), None)
    _redo_edges(conv, u_ref, acc_sc, seq, seq + n_ctx)
    xc = _silu(acc_sc[...])
    xcb = xc.astype(_BF16)
    te = ub.shape[1]
    own_block = (lax.broadcasted_iota(jnp.int32, (te, te), 0) // QKV_BLOCK
                 == lax.broadcasted_iota(jnp.int32, (te, te), 1) // QKV_BLOCK)
    dense = lambda w_ref: jnp.where(own_block, w_ref[...].astype(_F32), 0.0).astype(_BF16)
    q = jnp.dot(xcb, dense(wq_ref), preferred_element_type=_F32)
    k = jnp.dot(xcb, dense(wk_ref), preferred_element_type=_F32)
    v = jnp.dot(ub, dense(wv_ref), preferred_element_type=_F32)
    qb, kb, vb = q.astype(_BF16), k.astype(_BF16), v.astype(_BF16)
    xc_ref[...] = xcb
    q_ref[...] = qb
    k_ref[...] = (k * k_scale).astype(_BF16)
    v_ref[...] = vb
    part = jnp.dot(jnp.concatenate([qb, kb, vb], axis=1), wg_ref[...], preferred_element_type=_F32)

    @pl.when(j == 0)
    def _():
        g_ref[...] = part + gb_ref[...]

    @pl.when(j > 0)
    def _():
        g_ref[...] += part

    @pl.when(j == pl.num_programs(1) - 1)
    def _():
        g = g_ref[...]
        col = lax.broadcasted_iota(jnp.int32, g.shape, 1)
        log_sig = jnp.minimum(g, 0.0) - jnp.log1p(jnp.exp(-jnp.abs(g)))
        g_ref[...] = jnp.where(col < n_gate, g, log_sig)


def _blockdiag_rows(w, te):
    rows = w.reshape(w.shape[0] * QKV_BLOCK, QKV_BLOCK).astype(_BF16)
    return jnp.tile(rows, (1, te // QKV_BLOCK))


def _mlstm_pre(uz, conv_w, conv_b, wq, wk, wv, w_ig, b_ig, w_fg, b_fg, seq):
    bsz, t_all, e2 = uz.shape
    e = e2 // 2
    nh = w_ig.shape[-1]
    te = _tile(e, 256, LANES)
    nt = e // te
    ng = 4 * nh
    wg = jnp.concatenate([w_ig[0], w_ig[1], w_fg[0], w_fg[1]], axis=-1)
    wg = wg.reshape(3, nt, te, ng).transpose(1, 0, 2, 3).reshape(nt, 3 * te, ng).astype(_BF16)
    gb = jnp.concatenate([b_ig[0], b_ig[1], b_fg[0], b_fg[1]], axis=-1).reshape(1, ng)
    tok = pl.BlockSpec((None, t_all, te), lambda b, j: (b, 0, j))
    bd = pl.BlockSpec((te, te), lambda b, j: (j, 0))
    act = jax.ShapeDtypeStruct((bsz, t_all, e), _BF16)
    return pl.pallas_call(
        functools.partial(_mlstm_pre_kernel, seq=seq, n_ctx=t_all - seq,
                          k_scale=float((e // nh) ** -0.5), n_gate=2 * nh),
        grid=(bsz, nt),
        in_specs=[
            tok,
            pl.BlockSpec((CONV_W, te), lambda b, j: (0, j)),
            pl.BlockSpec((1, te), lambda b, j: (0, j)),
            bd, bd, bd,
            pl.BlockSpec((None, 3 * te, ng), lambda b, j: (j, 0, 0)),
            pl.BlockSpec((1, ng), lambda b, j: (0, 0)),
        ],
        out_specs=[tok, tok, tok, tok, pl.BlockSpec((None, t_all, ng), lambda b, j: (b, 0, 0))],
        out_shape=[act, act, act, act, jax.ShapeDtypeStruct((bsz, t_all, ng), _F32)],
        scratch_shapes=[pltpu.VMEM((t_all, te), _F32)],
        compiler_params=_cparams("parallel", "arbitrary"),
        name="mlstm_pre",
    )(uz, conv_w, conv_b.reshape(1, e), _blockdiag_rows(wq, te), _blockdiag_rows(wk, te),
      _blockdiag_rows(wv, te), wg, gb)


def _scan_kernel(q_ref, k_ref, v_ref, li_ref, lf_ref, *rest, rev):
    if rev:
        hf_ref, xc_ref, z_ref, nw_ref, sk_ref, o_ref, c_sc, cb_sc, n_sc, m_sc = rest
    else:
        o_ref, c_sc, cb_sc, n_sc, m_sc = rest

    @pl.when(pl.program_id(2) == 0)
    def _():
        c_sc[...] = jnp.zeros(c_sc.shape, _F32)
        cb_sc[...] = jnp.zeros(cb_sc.shape, _BF16)
        n_sc[...] = jnp.zeros(n_sc.shape, _F32)
        m_sc[...] = jnp.zeros(m_sc.shape, _F32)

    ln = q_ref.shape[0]
    dh = c_sc.shape[-1]
    ti = lax.broadcasted_iota(jnp.int32, (ln, ln), 0)
    si = lax.broadcasted_iota(jnp.int32, (ln, ln), 1)
    incl = si >= ti if rev else si <= ti
    incl_t = ti >= si if rev else ti <= si
    eye = si == ti
    for hh in range(c_sc.shape[0]):
        cols = slice(hh * dh, (hh + 1) * dh)
        h = _scan_chunk(q_ref[:, cols], k_ref[:, cols], v_ref[:, cols], li_ref[hh], lf_ref[hh],
                        incl, incl_t, eye, c_sc.at[hh], cb_sc.at[hh], n_sc.at[hh], m_sc.at[hh])
        if rev:
            h = h + hf_ref[:, cols].astype(_F32)
            mu = jnp.mean(h, axis=-1, keepdims=True)
            hc = h - mu
            var = jnp.mean(hc * hc, axis=-1, keepdims=True)
            hn = hc * lax.rsqrt(var + EPS) * nw_ref[:, cols]
            h = (hn + sk_ref[:, cols] * xc_ref[:, cols].astype(_F32)) * _silu(z_ref[:, cols].astype(_F32))
        o_ref[:, cols] = h.astype(o_ref.dtype)


def _scan_chunk(q, k, v, li_r, lf_r, incl, incl_t, eye, c_sc, cb_sc, n_sc, m_sc):
    ln = q.shape[0]
    lf_b = jnp.broadcast_to(lf_r, (ln, ln))
    li_b = jnp.broadcast_to(li_r, (ln, ln))
    b_col = jnp.sum(jnp.where(incl, lf_b, 0.0), axis=1, keepdims=True)
    lf_col = jnp.sum(jnp.where(eye, lf_b, 0.0), axis=1, keepdims=True)
    li_col = jnp.sum(jnp.where(eye, li_b, 0.0), axis=1, keepdims=True)
    b_row = jnp.sum(jnp.where(incl_t, lf_col, 0.0), axis=0, keepdims=True)
    b_end = jnp.sum(lf_r, axis=1, keepdims=True)
    m_old = m_sc[...]
    g_row = b_end - b_row + li_r
    g_col = b_end - b_col + li_col
    m_new = jnp.maximum(b_end + m_old, jnp.max(g_row, axis=1, keepdims=True))
    wg_col = jnp.exp(g_col - m_new)
    decay = jnp.exp(b_end + m_old - m_new)

    logw = jnp.where(incl, b_col - b_row + li_r, -jnp.inf)
    inter = b_col + m_old
    m_t = jnp.maximum(jnp.max(logw, axis=1, keepdims=True), inter)
    s = lax.dot_general(q, k, (((1,), (1,)), ((), ())), preferred_element_type=_F32) * jnp.exp(logw - m_t)
    w_inter = jnp.exp(inter - m_t)
    den = (jnp.sum(s, axis=1, keepdims=True)
           + w_inter * jnp.sum(q.astype(_F32) * n_sc[...], axis=1, keepdims=True))
    inv = 1.0 / jnp.maximum(jnp.abs(den), jnp.exp(-m_t))
    sb = s.astype(_BF16)
    kw = k.astype(_F32) * wg_col
    kwb = kw.astype(_BF16)
    n_sc[...] = decay * n_sc[...] + jnp.sum(kw, axis=0, keepdims=True)
    m_sc[...] = m_new
    dv = v.shape[1]
    step = min(dv, 2 * LANES)
    pieces = []
    for c0 in range(0, dv, step):
        cols = slice(c0, c0 + step)
        vc = v[:, cols]
        num = (jnp.dot(sb, vc, preferred_element_type=_F32)
               + w_inter * jnp.dot(q, cb_sc[:, cols], preferred_element_type=_F32))
        pieces.append(num * inv)
        c_new = decay * c_sc[:, cols] + lax.dot_general(kwb, vc, (((0,), (0,)), ((), ())),
                                                        preferred_element_type=_F32)
        c_sc[:, cols] = c_new
        cb_sc[:, cols] = c_new.astype(_BF16)
    return jnp.concatenate(pieces, axis=1)


def _mlstm_scan(q, k, v, gates, uz, xc, hnorm_w, skip, seq, nh):
    bsz, t_all, e = q.shape
    dh = e // nh
    n_ctx = t_all - seq
    ln = _tile(math.gcd(seq, n_ctx), SCAN_CHUNK, 8)
    nc = t_all // ln
    ncl = seq // ln
    ncc = n_ctx // ln
    g = gates.reshape(bsz, nc, ln, 2, 2, nh).transpose(3, 4, 0, 5, 1, 2).reshape(2, 2, bsz, nh, nc, 1, ln)
    act = jax.ShapeDtypeStruct((bsz, t_all, e), _BF16)

    def run(rev, extra_in, extra_specs):
        hb = SCAN_HEADS_PER_STEP[int(rev)]
        hb = hb if nh % hb == 0 else 1
        scratch = [pltpu.VMEM((hb, dh, dh), _F32), pltpu.VMEM((hb, dh, dh), _BF16),
                   pltpu.VMEM((hb, 1, dh), _F32), pltpu.VMEM((hb, 1, 1), _F32)]

        def chunk(s):
            if rev:
                return jnp.where(s < ncc, ncl + ncc - 1 - s, ncl - 1 - (s - ncc))
            return jnp.where(s < ncc, ncl + s, s - ncc)

        tok = lambda col0: pl.BlockSpec((None, ln, hb * dh), lambda b, hh, s: (b, chunk(s), col0 + hh))
        gate = pl.BlockSpec((None, hb, None, 1, ln), lambda b, hh, s: (b, hh, chunk(s), 0, 0))
        return pl.pallas_call(
            functools.partial(_scan_kernel, rev=rev),
            grid=(bsz, nh // hb, nc),
            in_specs=[tok(0), tok(0), tok(0), gate, gate, *extra_specs(tok, hb)],
            out_specs=tok(0),
            out_shape=act,
            scratch_shapes=scratch,
            compiler_params=_cparams("parallel", "parallel", "arbitrary"),
            name="mlstm_scan_rev" if rev else "mlstm_scan_fwd",
        )(q, k, v, g[0, int(rev)], g[1, int(rev)], *extra_in)

    h_fwd = run(False, (), lambda tok, hb: ())
    vec = lambda hb: pl.BlockSpec((1, hb * dh), lambda b, hh, s: (0, hh))
    return run(True, (h_fwd, xc, uz, hnorm_w.reshape(1, e), skip.reshape(1, e)),
               lambda tok, hb: (tok(0), tok(0), tok(nh // hb), vec(hb), vec(hb)))


def _mlstm_branch(uz, conv_w, conv_b, wq, wk, wv, w_ig, b_ig, w_fg, b_fg, hnorm_w, skip, seq):
    nh = w_ig.shape[-1]
    xc, q, k, v, gates = _mlstm_pre(uz, conv_w, conv_b, wq, wk, wv, w_ig, b_ig, w_fg, b_fg, seq)
    return _mlstm_scan(q, k, v, gates, uz, xc, hnorm_w, skip, seq, nh)


def _pool_kernel(u_ref, o_ref, *, seq, n_ctx, tiles_per_group):
    grp = pl.program_id(1) // tiles_per_group
    for gi, w in enumerate(POOL_WINDOWS):
        lo = w // 2
        hi = w - 1 - lo
        assert hi == lo - 1 and lo <= EDGE_ROWS

        def delta(u, bounds, w=w, lo=lo, hi=hi):
            trail = u
            lead = u
            span = 1
            while span < lo:
                trail = trail + _shift_rows(trail, -span, bounds)
                if bounds is not None:
                    lead = lead + _shift_rows(lead, span, bounds)
                span *= 2
            if bounds is None:
                lead = _shift_rows(trail, lo - 1, None) if lo > 1 else u
                return (_shift_rows(trail, -1, None) + lead) * (1.0 / w) - u
            win = _shift_rows(trail, -1, bounds) + lead
            pos, seg_len = bounds
            cnt = jnp.minimum(pos + hi + 1, seg_len) - jnp.maximum(pos - lo, 0)
            return win / cnt.astype(_F32) - u

        @pl.when(grp == gi)
        def _(delta=delta):
            o_ref[...] = delta(u_ref[...].astype(_F32), None).astype(o_ref.dtype)
            _redo_edges(delta, u_ref, o_ref, seq, seq + n_ctx)


def _pool_group_linear(a, w_grp, uz, scale):
    bsz, t_all, e = a.shape
    ng, gw, _ = w_grp.shape
    tm = _tile(t_all, 1088, 16)
    tn = _tile(gw, 1024, LANES)
    npg = gw // tn

    def epilogue(acc, pids, z_ref, sc_ref):
        return acc * sc_ref[...] * _silu(z_ref[...].astype(_F32))

    extras = [uz, scale.reshape(1, e)]
    especs = [pl.BlockSpec((None, tm, tn), lambda b, g, j, i, k: (b, i, (e // tn) + g * npg + j)),
              pl.BlockSpec((1, tn), lambda b, g, j, i, k: (0, g * npg + j))]
    return _matmul(
        (bsz, ng, npg, t_all // tm, 1),
        a, pl.BlockSpec((None, tm, gw), lambda b, g, j, i, k: (b, i, g)),
        w_grp, pl.BlockSpec((None, gw, tn), lambda b, g, j, i, k: (g, 0, j)),
        jax.ShapeDtypeStruct((bsz, t_all, e), _BF16),
        pl.BlockSpec((None, tm, tn), lambda b, g, j, i, k: (b, i, g * npg + j)),
        extras=tuple(extras), extra_specs=tuple(especs), epilogue=epilogue, name="pool_group")


def _pool_branch(uz, w_grp, scale, seq):
    bsz, t_all, e2 = uz.shape
    e = e2 // 2
    te = _tile(e // N_GROUPS, 256, LANES)
    tok = pl.BlockSpec((None, t_all, te), lambda b, j: (b, 0, j))
    dlt = pl.pallas_call(
        functools.partial(_pool_kernel, seq=seq, n_ctx=t_all - seq, tiles_per_group=e // N_GROUPS // te),
        grid=(bsz, e // te),
        in_specs=[tok],
        out_specs=tok,
        out_shape=jax.ShapeDtypeStruct((bsz, t_all, e), _BF16),
        compiler_params=_cparams("parallel", "parallel"),
        name="pool",
    )(uz)
    return _pool_group_linear(dlt, w_grp, uz, scale)


def _cos_sin(n_out, n_in, period):
    ang = (np.outer(np.arange(n_out), np.arange(n_in)) % period) * (2.0 * np.pi / period)
    return np.cos(ang), np.sin(ang)


@functools.lru_cache(maxsize=None)
def _chan_table(gw):
    c, s = _cos_sin(gw, gw, gw)
    return np.concatenate([c, s], axis=0) * gw ** -0.5


@functools.lru_cache(maxsize=None)
def _time_tables(seq, n_ctx):
    n1 = max(f for f in range(1, math.isqrt(seq) + 1) if seq % f == 0)
    n2 = seq // n1
    g = SUBLANES
    assert n1 % g == 0 and n2 % g == 0
    eye = np.eye(g)
    c1, s1 = _cos_sin(n1, n1, n1)
    base = np.block([[c1, -s1], [-s1, -c1]]) * seq ** -0.5
    stage1 = np.einsum('ab,rkst->arkstb', eye, base.reshape(2, n1, 2, n1)).reshape(g * 2 * n1, 2 * n1 * g)
    twc, tws = _cos_sin(n2, n1, seq)
    c2, s2 = _cos_sin(n2, n2, n2)
    cs2 = np.concatenate([c2, s2], axis=1)
    stage2 = np.einsum('ab,krt->kartb', eye, cs2.reshape(n2, 2, n2)).reshape(n2 * g, 2 * n2 * g)
    cc, sc = _cos_sin(n_ctx, n_ctx, n_ctx)
    ctx_tab = np.concatenate([cc, -sc], axis=1) * n_ctx ** -0.5
    lanes = lambda a: np.repeat(a[:, :, None], LANES, axis=2)
    return n1, n2, stage1, lanes(twc), lanes(tws), stage2, ctx_tab


def _time_dft_kernel(y1_ref, y2_ref, z_ref, w1_ref, twc_ref, tws_ref, w2_ref, wc_ref, o_ref,
                     y1f, y2f, ar, ai, out, *, seq, n1, n2):
    g = SUBLANES
    tc = o_ref.shape[-1]
    wide = lambda tab: jnp.concatenate([tab] * (tc // LANES), axis=1)
    y1f[...] = y1_ref[:seq, :].astype(_F32).reshape(y1f.shape)
    y2f[...] = y2_ref[:seq, :].astype(_F32).reshape(y2f.shape)
    for th in range(n2 // g):
        rhs = jnp.concatenate([y1f[:, th].reshape(n1 * g, tc), y2f[:, th].reshape(n1 * g, tc)], axis=0)
        res = jnp.dot(w1_ref[...], rhs.astype(_BF16), preferred_element_type=_F32)
        for tl in range(g):
            t2 = th * g + tl
            a_r = res[tl * 2 * n1:tl * 2 * n1 + n1]
            a_i = res[tl * 2 * n1 + n1:(tl + 1) * 2 * n1]
            c, s = wide(twc_ref[t2]), wide(tws_ref[t2])
            ar[t2] = (a_r * c + a_i * s).reshape(ar.shape[1:])
            ai[t2] = (a_i * c - a_r * s).reshape(ai.shape[1:])
    for kh in range(n1 // g):
        rhs = jnp.concatenate([ar[:, kh].reshape(n2 * g, tc), ai[:, kh].reshape(n2 * g, tc)], axis=0)
        res = jnp.dot(w2_ref[...], rhs.astype(_BF16), preferred_element_type=_F32)
        out[:, kh] = res.reshape(n2, g, tc)
    gate = _silu(z_ref[:seq, :].astype(_F32))
    o_ref[:seq, :] = (out[...].reshape(seq, tc) * gate).astype(o_ref.dtype)
    rhs = jnp.concatenate([y1_ref[seq:, :], y2_ref[seq:, :]], axis=0)
    res = jnp.dot(wc_ref[...], rhs, preferred_element_type=_F32)
    o_ref[seq:, :] = (res * _silu(z_ref[seq:, :].astype(_F32))).astype(o_ref.dtype)


def _fourier_branch(uz, w_grp, seq):
    bsz, t_all, e2 = uz.shape
    n_ctx = t_all - seq
    e = e2 // 2
    gw = e // N_GROUPS
    tab = jnp.asarray(_chan_table(gw), dtype=_BF16)
    tw = _tile(gw, 1024, LANES)
    nrow = gw // tw
    wcomb = _matmul(
        (N_GROUPS, gw // tw, 2 * nrow, 1),
        tab, pl.BlockSpec((tw, gw), lambda g, j, i, k: (i, 0)),
        w_grp, pl.BlockSpec((None, gw, tw), lambda g, j, i, k: (g, 0, j)),
        jax.ShapeDtypeStruct((N_GROUPS, gw, 2 * gw), _BF16),
        pl.BlockSpec((None, tw, tw), lambda g, j, i, k: (g, i % nrow, (i // nrow) * (gw // tw) + j)),
        name="fourier_weight")
    tm = _tile(t_all, 1088, 16)
    tn = _tile(gw, 1024, LANES)
    y = _matmul(
        (bsz, N_GROUPS, 2 * gw // tn, t_all // tm, 1),
        uz, pl.BlockSpec((None, tm, gw), lambda b, g, j, i, k: (b, i, g)),
        wcomb, pl.BlockSpec((None, gw, tn), lambda b, g, j, i, k: (g, 0, j)),
        jax.ShapeDtypeStruct((bsz, t_all, 2 * e), _BF16),
        pl.BlockSpec((None, tm, tn), lambda b, g, j, i, k: (b, i, g * (2 * gw // tn) + j)),
        name="dft_chan")
    n1, n2, stage1, twc, tws, stage2, ctx_tab = _time_tables(seq, n_ctx)
    tc = _tile(gw, 256, LANES)
    cpg = gw // tc

    def ycol(part):
        return lambda b, c: (b, 0, (2 * (c // cpg) + part) * cpg + c % cpg)

    tok = lambda imap: pl.BlockSpec((None, t_all, tc), imap)
    full = lambda a: pl.BlockSpec(a.shape, lambda b, c: (0,) * a.ndim, pipeline_mode=pl.Buffered(1))
    tabs = [jnp.asarray(stage1, dtype=_BF16), jnp.asarray(twc, dtype=_F32), jnp.asarray(tws, dtype=_F32),
            jnp.asarray(stage2, dtype=_BF16), jnp.asarray(ctx_tab, dtype=_BF16)]
    by_t1 = pltpu.VMEM((n1, n2 // SUBLANES, SUBLANES, tc), _F32)
    by_t2 = pltpu.VMEM((n2, n1 // SUBLANES, SUBLANES, tc), _F32)
    return pl.pallas_call(
        functools.partial(_time_dft_kernel, seq=seq, n1=n1, n2=n2),
        grid=(bsz, e // tc),
        in_specs=[tok(ycol(0)), tok(ycol(1)), tok(lambda b, c: (b, 0, e // tc + c))] + [full(t) for t in tabs],
        out_specs=tok(lambda b, c: (b, 0, c)),
        out_shape=jax.ShapeDtypeStruct((bsz, t_all, e), _BF16),
        scratch_shapes=[by_t1, by_t1, by_t2, by_t2, by_t2],
        compiler_params=_cparams("parallel", "parallel"),
        name="dft_time",
    )(y, y, uz, *tabs)


def kernel(x, c, ctx, c_ctx, ada_w, ada_b, norm_g, final_g, a_w_in, a_conv_w, a_conv_b, a_wq, a_wk, a_wv,
           a_w_ig, a_b_ig, a_w_fg, a_b_fg, a_hnorm_w, a_skip, a_w_out, b_w_in, b_w_grp, b_scale, b_w_out,
           c_w_in, c_w_grp, c_w_out):
    bsz, seq, d = x.shape
    depth = ada_w.shape[0]
    n_rows = 8
    assert bsz + 1 <= n_rows
    rows = jnp.concatenate([c, c_ctx[None], jnp.zeros((n_rows - bsz - 1, d), c.dtype)], axis=0)
    mods = _ada(rows, ada_w, ada_b).reshape(depth, n_rows, 1, 3 * d)
    xs = _embed(x, ctx)
    for i in range(depth):
        kind = i % N_MIXERS
        j = i // N_MIXERS
        h = _prenorm(xs, norm_g[i], mods[i], seq)
        w_in, w_out = ((a_w_in, a_w_out), (b_w_in, b_w_out), (c_w_in, c_w_out))[kind]
        uz = _in_proj(h, w_in, j)
        if kind == 0:
            a = _mlstm_branch(uz, a_conv_w[j], a_conv_b[j], a_wq[j], a_wk[j], a_wv[j],
                              a_w_ig[j], a_b_ig[j], a_w_fg[j], a_b_fg[j], a_hnorm_w[j], a_skip[j], seq)
        elif kind == 1:
            a = _pool_branch(uz, b_w_grp[j], b_scale[j], seq)
        else:
            a = _fourier_branch(uz, c_w_grp[j], seq)
        xs = _out_proj(a, w_out, j, xs, mods[i], seq)
    return _final_norm(xs, final_g, seq)
```

```python
import functools
import math

import jax
import jax.numpy as jnp
import numpy as np
from jax import lax
from jax.experimental import pallas as pl
from jax.experimental.pallas import tpu as pltpu

GRID_W = 64
N_MIXERS = 3
QKV_BLOCK = 4
CONV_W = 4
POOL_WINDOWS = (2, 4, 8, 16)
N_GROUPS = 4
POS_BASE = 10000.0
EPS = 1e-6

LANES = 128
SUBLANES = 8
VMEM_LIMIT_BYTES = 56 * 1024 * 1024
SCAN_CHUNK = 256
SCAN_HEADS_PER_STEP = (4, 2)

_F32 = jnp.float32
_BF16 = jnp.bfloat16


def _cparams(*sem):
    return pltpu.CompilerParams(dimension_semantics=sem, vmem_limit_bytes=VMEM_LIMIT_BYTES)


def _tile(n, target, align):
    best = None
    for t in range(align, min(n, target) + 1, align):
        if n % t == 0:
            best = t
    return n if best is None else best


def _silu(v):
    half = 0.5 * v
    return half + half * jnp.tanh(half)


def _matmul(grid, a, a_spec, b, b_spec, out_shape, out_spec, extras=(), extra_specs=(), epilogue=None,
            name="matmul"):
    n_extra = len(extras)
    if epilogue is None:
        epilogue = lambda acc, pids: acc

    def body(*refs):
        a_ref, b_ref = refs[0], refs[1]
        ex = refs[2:2 + n_extra]
        o_ref = refs[2 + n_extra]
        pids = tuple(pl.program_id(ax) for ax in range(len(grid)))
        acc = jnp.dot(a_ref[...].astype(_BF16), b_ref[...].astype(_BF16), preferred_element_type=_F32)
        o_ref[...] = epilogue(acc, pids, *ex).astype(o_ref.dtype)

    return pl.pallas_call(
        body,
        grid=grid,
        in_specs=[a_spec, b_spec, *extra_specs],
        out_specs=out_spec,
        out_shape=out_shape,
        compiler_params=_cparams(*(("parallel",) * len(grid))),
        name=name,
    )(a, b, *extras)


def _proj(a, w, layer, tm, tn, out_dtype, name, extras=(), extra_blocks=(), epilogue=None, side=None):
    bsz, t_all, k = a.shape
    n = w.shape[2]
    nt, mt = n // tn, t_all // tm
    pieces = bsz * mt
    tkp = k // pieces
    assert tkp * pieces == k and tkp % 16 == 0
    n_extra = len(extras)
    if epilogue is None:
        epilogue = lambda acc, pids: acc
    if side is not None:
        s_rows, s_w, s_b, s_layer = side
        tns = s_w.shape[2] // nt
        assert s_w.shape[1] == k and tns * nt == s_w.shape[2] and tns % LANES == 0

    def body(*refs):
        a_ref, w_ref = refs[0], refs[1]
        ex = refs[2:2 + n_extra]
        rest = refs[2 + n_extra:]
        if side is not None:
            sr_ref, sw_ref, sb_ref, o_ref, so_ref, w_bf = rest
        else:
            o_ref, w_bf = rest
        pids = tuple(pl.program_id(ax) for ax in range(3))
        jj = pids[0]
        piece = pids[1] * mt + pids[2]

        def stage():
            rows = pl.ds(pl.multiple_of(piece * tkp, tkp), tkp)
            w_bf[jj % 2, rows, :] = w_ref[...].astype(_BF16)

        @pl.when(jj == 0)
        def _():
            stage()

        @pl.when(jj > 0)
        def _():
            stage()
            acc = jnp.dot(a_ref[...], w_bf[(jj + 1) % 2], preferred_element_type=_F32)
            o_ref[...] = epilogue(acc, pids, *ex).astype(o_ref.dtype)
            if side is not None:
                part = jnp.dot(_silu(sr_ref[...]).astype(_BF16), sw_ref[...].astype(_BF16),
                               preferred_element_type=_F32)

                @pl.when(piece == 0)
                def _():
                    so_ref[...] = part + sb_ref[...]

                @pl.when(piece > 0)
                def _():
                    so_ref[...] += part

    def a_map(jj, b, i):
        return (jnp.where(jj > 0, b, 0), jnp.where(jj > 0, i, 0), 0)

    def o_map(jj, b, i):
        return (jnp.where(jj > 0, b, 0), jnp.where(jj > 0, i, 0), jnp.maximum(jj - 1, 0))

    def w_map(jj, b, i):
        last = jj == nt
        return (layer, jnp.where(last, pieces - 1, b * mt + i), jnp.minimum(jj, nt - 1))

    in_specs = [pl.BlockSpec((None, tm, k), a_map), pl.BlockSpec((None, tkp, tn), w_map),
                *[spec(o_map) for spec in extra_blocks]]
    out_specs = [pl.BlockSpec((None, tm, tn), o_map)]
    out_shape = [jax.ShapeDtypeStruct((bsz, t_all, n), out_dtype)]
    operands = [a, w, *extras]
    if side is not None:
        nr = s_rows.shape[0]
        tile_s = lambda jj: jnp.maximum(jj - 1, 0)
        piece_s = lambda jj, b, i: jnp.where(jj > 0, b * mt + i, 0)
        in_specs += [pl.BlockSpec((nr, tkp), lambda jj, b, i: (0, piece_s(jj, b, i))),
                     pl.BlockSpec((None, tkp, tns), lambda jj, b, i: (s_layer, piece_s(jj, b, i), tile_s(jj))),
                     pl.BlockSpec((None, 1, tns), lambda jj, b, i: (s_layer, 0, tile_s(jj)))]
        out_specs.append(pl.BlockSpec((nr, tns), lambda jj, b, i: (0, tile_s(jj))))
        out_shape.append(jax.ShapeDtypeStruct((nr, s_w.shape[2]), _F32))
        operands += [s_rows, s_w, s_b]
    outs = pl.pallas_call(
        body,
        grid=(nt + 1, bsz, mt),
        in_specs=in_specs,
        out_specs=out_specs,
        out_shape=out_shape,
        scratch_shapes=[pltpu.VMEM((2, k, tn), _BF16)],
        compiler_params=_cparams("arbitrary", "arbitrary", "arbitrary"),
        name=name,
    )(*operands)
    return outs[0] if side is None else outs


def _ada_kernel(s_ref, w_ref, b_ref, o_ref):
    s = _silu(s_ref[...]).astype(_BF16)
    acc = jnp.dot(s, w_ref[...].astype(_BF16), preferred_element_type=_F32)
    o_ref[...] = acc + b_ref[...]


def _ada(rows, ada_w, ada_b, depth):
    _, d, n3 = ada_w.shape
    nr = rows.shape[0]
    tn = _tile(n3, 1024, LANES)
    return pl.pallas_call(
        _ada_kernel,
        grid=(depth, n3 // tn),
        in_specs=[
            pl.BlockSpec((nr, d), lambda i, n: (0, 0)),
            pl.BlockSpec((None, d, tn), lambda i, n: (i, 0, n)),
            pl.BlockSpec((None, 1, tn), lambda i, n: (i, 0, n)),
        ],
        out_specs=pl.BlockSpec((None, nr, tn), lambda i, n: (i, 0, n)),
        out_shape=jax.ShapeDtypeStruct((depth, nr, n3), _F32),
        compiler_params=_cparams("parallel", "parallel"),
        name="ada",
    )(rows, ada_w, ada_b.reshape(-1, 1, n3))


def _embed_kernel(x_ref, c_ref, re_ref, ce_ref, o_ref, *, n_lat, half):
    r = pl.program_id(1)

    @pl.when(r < n_lat)
    def _():
        for i in range(re_ref.shape[0]):
            rows = slice(i * GRID_W, (i + 1) * GRID_W)
            o_ref[rows, :half] = x_ref[rows, :half] + re_ref[i:i + 1, :]
            o_ref[rows, half:] = x_ref[rows, half:] + ce_ref[...]

    @pl.when(r >= n_lat)
    def _():
        o_ref[...] = c_ref[...]


def _embed(x, ctx):
    bsz, seq, d = x.shape
    n_ctx = ctx.shape[1]
    n_rows = seq // GRID_W
    rb = max(f for f in (4, 2, 1) if n_rows % f == 0 and n_ctx % (f * GRID_W) == 0)
    tb = rb * GRID_W
    n_lat = seq // tb
    half = d // 2
    quarter = d // 4
    omega = 1.0 / (POS_BASE ** (jnp.arange(quarter, dtype=_F32) / quarter))

    def axis_emb(p):
        ang = p[:, None] * omega[None, :]
        return jnp.concatenate([jnp.sin(ang), jnp.cos(ang)], axis=-1)

    row_emb = axis_emb(jnp.arange(n_rows, dtype=_F32)).reshape(n_lat, rb, half)
    col_emb = axis_emb(jnp.arange(GRID_W, dtype=_F32))
    return pl.pallas_call(
        functools.partial(_embed_kernel, n_lat=n_lat, half=half),
        grid=(bsz, n_lat + n_ctx // tb),
        in_specs=[
            pl.BlockSpec((None, tb, d), lambda b, r: (b, jnp.minimum(r, n_lat - 1), 0)),
            pl.BlockSpec((None, tb, d), lambda b, r: (b, jnp.maximum(r - n_lat, 0), 0)),
            pl.BlockSpec((None, rb, half), lambda b, r: (jnp.minimum(r, n_lat - 1), 0, 0)),
            pl.BlockSpec((GRID_W, half), lambda b, r: (0, 0)),
        ],
        out_specs=pl.BlockSpec((None, tb, d), lambda b, r: (b, r, 0)),
        out_shape=jax.ShapeDtypeStruct((bsz, seq + n_ctx, d), x.dtype),
        compiler_params=_cparams("parallel", "parallel"),
        name="embed",
    )(x, ctx, row_emb, col_emb)


def _prenorm_kernel(x_ref, g_ref, sh_ref, sc_ref, o_ref):
    x = x_ref[...]
    y = x * lax.rsqrt(jnp.mean(x * x, axis=-1, keepdims=True) + EPS)
    o_ref[...] = (y * g_ref[...] * (1.0 + sc_ref[...]) + sh_ref[...]).astype(o_ref.dtype)


def _prenorm(xs, g, mods, seq):
    bsz, t_all, d = xs.shape
    tb = _tile(math.gcd(seq, t_all - seq), 256, 8)
    n_lat = seq // tb

    def mrow(b, j):
        return jnp.where(j < n_lat, b, bsz)

    return pl.pallas_call(
        _prenorm_kernel,
        grid=(bsz, t_all // tb),
        in_specs=[
            pl.BlockSpec((None, tb, d), lambda b, j: (b, j, 0)),
            pl.BlockSpec((1, d), lambda b, j: (0, 0)),
            pl.BlockSpec((None, 1, d), lambda b, j: (mrow(b, j), 0, 0)),
            pl.BlockSpec((None, 1, d), lambda b, j: (mrow(b, j), 0, 1)),
        ],
        out_specs=pl.BlockSpec((None, tb, d), lambda b, j: (b, j, 0)),
        out_shape=jax.ShapeDtypeStruct((bsz, t_all, d), _BF16),
        compiler_params=_cparams("parallel", "parallel"),
        name="prenorm",
    )(xs, g.reshape(1, d), mods, mods)


def _final_norm_kernel(x_ref, g_ref, o_ref):
    x = x_ref[...]
    y = x * lax.rsqrt(jnp.mean(x * x, axis=-1, keepdims=True) + EPS)
    o_ref[...] = y * g_ref[...]


def _final_norm(xs, g, seq):
    bsz, _, d = xs.shape
    tb = _tile(seq, 512, 8)
    return pl.pallas_call(
        _final_norm_kernel,
        grid=(bsz, seq // tb),
        in_specs=[
            pl.BlockSpec((None, tb, d), lambda b, j: (b, j, 0)),
            pl.BlockSpec((1, d), lambda b, j: (0, 0)),
        ],
        out_specs=pl.BlockSpec((None, tb, d), lambda b, j: (b, j, 0)),
        out_shape=jax.ShapeDtypeStruct((bsz, seq, d), xs.dtype),
        compiler_params=_cparams("parallel", "parallel"),
        name="final_norm",
    )(xs, g.reshape(1, d))


def _in_proj(h, w, layer, side=None):
    t_all, n = h.shape[1], w.shape[2]
    return _proj(h, w, layer, _tile(t_all, 1088, 16), _tile(n, 1024, LANES), _BF16, "in_proj", side=side)


def _out_proj(a, w, layer, xs, mods, seq):
    bsz, t_all, _ = a.shape
    d = w.shape[2]
    tm = _tile(t_all, 544, 16)
    tn = _tile(d, 512, LANES)
    g_off = 2 * (d // tn)

    def epilogue(acc, pids, res_ref, gl_ref, gc_ref):
        row = pids[2] * tm + lax.broadcasted_iota(jnp.int32, acc.shape, 0)
        gate = jnp.where(row < seq, gl_ref[...], gc_ref[...])
        return res_ref[...] + gate * acc

    def gate_block(row_of):
        return lambda om: pl.BlockSpec((None, 1, tn), lambda jj, b, i: (row_of(b), 0, g_off + om(jj, b, i)[2]))

    return _proj(a, w, layer, tm, tn, xs.dtype, "out_proj", extras=(xs, mods, mods),
                 extra_blocks=(lambda om: pl.BlockSpec((None, tm, tn), om),
                               gate_block(lambda b: b), gate_block(lambda b: bsz)),
                 epilogue=epilogue)


EDGE_ROWS = 16
EDGE_HALO = 16


def _segment_pos(shape, seq, n_ctx, row0):
    t = row0 + lax.broadcasted_iota(jnp.int32, shape, 0)
    is_lat = t < seq
    return jnp.where(is_lat, t, t - seq), jnp.where(is_lat, seq, n_ctx)


def _shift_rows(u, k, bounds=None):
    rolled = pltpu.roll(u, (-k) % u.shape[0], axis=0)
    if bounds is None:
        return rolled
    pos, seg_len = bounds
    valid = (pos >= -k) if k < 0 else (pos < seg_len - k)
    return jnp.where(valid, rolled, 0.0)


def _edge_windows(seq, t_all):
    p, h = EDGE_ROWS, EDGE_HALO
    assert seq >= p + h and t_all - seq >= 2 * (p + h)
    return ((0, p + h, 0, p), (seq - p - h, 2 * (p + h), h, 2 * p), (t_all - p - h, p + h, h, p))


def _redo_edges(fn, u_ref, dst_ref, seq, t_all):
    for row0, rows, off, n in _edge_windows(seq, t_all):
        uw = u_ref[row0:row0 + rows, :].astype(_F32)
        res = fn(uw, _segment_pos(uw.shape, seq, t_all - seq, row0))
        dst_ref[row0 + off:row0 + off + n, :] = res[off:off + n].astype(dst_ref.dtype)


def _mlstm_pre_kernel(u_ref, cw_ref, cb_ref, wq_ref, wk_ref, wv_ref, wg_ref, gb_ref,
                      xc_ref, q_ref, k_ref, v_ref, g_ref, acc_sc, *, seq, n_ctx, k_scale, n_gate):
    j = pl.program_id(1)
    ub = u_ref[...]
    left = (CONV_W - 1) // 2

    def conv(u, bounds):
        acc = cb_ref[...] + u * cw_ref[left:left + 1, :]
        for jj in range(CONV_W):
            if jj != left:
                acc = acc + _shift_rows(u, jj - left, bounds) * cw_ref[jj:jj + 1, :]
        return acc

    acc_sc[...] = conv(ub.astype(_F32), None)
    _redo_edges(conv, u_ref, acc_sc, seq, seq + n_ctx)
    xc = _silu(acc_sc[...])
    xcb = xc.astype(_BF16)
    te = ub.shape[1]
    own_block = (lax.broadcasted_iota(jnp.int32, (te, te), 0) // QKV_BLOCK
                 == lax.broadcasted_iota(jnp.int32, (te, te), 1) // QKV_BLOCK)
    dense = lambda w_ref: jnp.where(own_block, w_ref[...].astype(_F32), 0.0).astype(_BF16)
    q = jnp.dot(xcb, dense(wq_ref), preferred_element_type=_F32)
    k = jnp.dot(xcb, dense(wk_ref), preferred_element_type=_F32)
    v = jnp.dot(ub, dense(wv_ref), preferred_element_type=_F32)
    qb, kb, vb = q.astype(_BF16), k.astype(_BF16), v.astype(_BF16)
    xc_ref[...] = xcb
    q_ref[...] = qb
    k_ref[...] = (k * k_scale).astype(_BF16)
    v_ref[...] = vb
    part = jnp.dot(jnp.concatenate([qb, kb, vb], axis=1), wg_ref[...], preferred_element_type=_F32)

    @pl.when(j == 0)
    def _():
        g_ref[...] = part + gb_ref[...]

    @pl.when(j > 0)
    def _():
        g_ref[...] += part

    @pl.when(j == pl.num_programs(1) - 1)
    def _():
        g = g_ref[...]
        col = lax.broadcasted_iota(jnp.int32, g.shape, 1)
        log_sig = jnp.minimum(g, 0.0) - jnp.log1p(jnp.exp(-jnp.abs(g)))
        g_ref[...] = jnp.where(col < n_gate, g, log_sig)


def _blockdiag_rows(w, te):
    rows = w.reshape(w.shape[0] * QKV_BLOCK, QKV_BLOCK).astype(_BF16)
    return jnp.tile(rows, (1, te // QKV_BLOCK))


def _mlstm_pre(uz, conv_w, conv_b, wq, wk, wv, w_ig, b_ig, w_fg, b_fg, seq):
    bsz, t_all, e2 = uz.shape
    e = e2 // 2
    nh = w_ig.shape[-1]
    te = _tile(e, 256, LANES)
    nt = e // te
    ng = 4 * nh
    wg = jnp.concatenate([w_ig[0], w_ig[1], w_fg[0], w_fg[1]], axis=-1)
    wg = wg.reshape(3, nt, te, ng).transpose(1, 0, 2, 3).reshape(nt, 3 * te, ng).astype(_BF16)
    gb = jnp.concatenate([b_ig[0], b_ig[1], b_fg[0], b_fg[1]], axis=-1).reshape(1, ng)
    tok = pl.BlockSpec((None, t_all, te), lambda b, j: (b, 0, j))
    bd = pl.BlockSpec((te, te), lambda b, j: (j, 0))
    act = jax.ShapeDtypeStruct((bsz, t_all, e), _BF16)
    return pl.pallas_call(
        functools.partial(_mlstm_pre_kernel, seq=seq, n_ctx=t_all - seq,
                          k_scale=float((e // nh) ** -0.5), n_gate=2 * nh),
        grid=(bsz, nt),
        in_specs=[
            tok,
            pl.BlockSpec((CONV_W, te), lambda b, j: (0, j)),
            pl.BlockSpec((1, te), lambda b, j: (0, j)),
            bd, bd, bd,
            pl.BlockSpec((None, 3 * te, ng), lambda b, j: (j, 0, 0)),
            pl.BlockSpec((1, ng), lambda b, j: (0, 0)),
        ],
        out_specs=[tok, tok, tok, tok, pl.BlockSpec((None, t_all, ng), lambda b, j: (b, 0, 0))],
        out_shape=[act, act, act, act, jax.ShapeDtypeStruct((bsz, t_all, ng), _F32)],
        scratch_shapes=[pltpu.VMEM((t_all, te), _F32)],
        compiler_params=_cparams("parallel", "arbitrary"),
        name="mlstm_pre",
    )(uz, conv_w, conv_b.reshape(1, e), _blockdiag_rows(wq, te), _blockdiag_rows(wk, te),
      _blockdiag_rows(wv, te), wg, gb)


def _scan_kernel(q_ref, k_ref, v_ref, li_ref, lf_ref, *rest, rev):
    if rev:
        hf_ref, xc_ref, z_ref, nw_ref, sk_ref, o_ref, c_sc, cb_sc, n_sc, m_sc = rest
    else:
        o_ref, c_sc, cb_sc, n_sc, m_sc = rest

    @pl.when(pl.program_id(2) == 0)
    def _():
        c_sc[...] = jnp.zeros(c_sc.shape, _F32)
        cb_sc[...] = jnp.zeros(cb_sc.shape, _BF16)
        n_sc[...] = jnp.zeros(n_sc.shape, _F32)
        m_sc[...] = jnp.zeros(m_sc.shape, _F32)

    ln = q_ref.shape[0]
    dh = c_sc.shape[-1]
    ti = lax.broadcasted_iota(jnp.int32, (ln, ln), 0)
    si = lax.broadcasted_iota(jnp.int32, (ln, ln), 1)
    incl = si >= ti if rev else si <= ti
    incl_t = ti >= si if rev else ti <= si
    eye = si == ti
    for hh in range(c_sc.shape[0]):
        cols = slice(hh * dh, (hh + 1) * dh)
        h = _scan_chunk(q_ref[:, cols], k_ref[:, cols], v_ref[:, cols], li_ref[hh], lf_ref[hh],
                        incl, incl_t, eye, c_sc.at[hh], cb_sc.at[hh], n_sc.at[hh], m_sc.at[hh])
        if rev:
            h = h + hf_ref[:, cols].astype(_F32)
            mu = jnp.mean(h, axis=-1, keepdims=True)
            hc = h - mu
            var = jnp.mean(hc * hc, axis=-1, keepdims=True)
            hn = hc * lax.rsqrt(var + EPS) * nw_ref[:, cols]
            h = (hn + sk_ref[:, cols] * xc_ref[:, cols].astype(_F32)) * _silu(z_ref[:, cols].astype(_F32))
        o_ref[:, cols] = h.astype(o_ref.dtype)


def _scan_chunk(q, k, v, li_r, lf_r, incl, incl_t, eye, c_sc, cb_sc, n_sc, m_sc):
    ln = q.shape[0]
    lf_b = jnp.broadcast_to(lf_r, (ln, ln))
    li_b = jnp.broadcast_to(li_r, (ln, ln))
    b_col = jnp.sum(jnp.where(incl, lf_b, 0.0), axis=1, keepdims=True)
    lf_col = jnp.sum(jnp.where(eye, lf_b, 0.0), axis=1, keepdims=True)
    li_col = jnp.sum(jnp.where(eye, li_b, 0.0), axis=1, keepdims=True)
    b_row = jnp.sum(jnp.where(incl_t, lf_col, 0.0), axis=0, keepdims=True)
    b_end = jnp.sum(lf_r, axis=1, keepdims=True)
    m_old = m_sc[...]
    g_row = b_end - b_row + li_r
    g_col = b_end - b_col + li_col
    m_new = jnp.maximum(b_end + m_old, jnp.max(g_row, axis=1, keepdims=True))
    wg_col = jnp.exp(g_col - m_new)
    decay = jnp.exp(b_end + m_old - m_new)

    logw = jnp.where(incl, b_col - b_row + li_r, -jnp.inf)
    inter = b_col + m_old
    m_t = jnp.maximum(jnp.max(logw, axis=1, keepdims=True), inter)
    s = lax.dot_general(q, k, (((1,), (1,)), ((), ())), preferred_element_type=_F32) * jnp.exp(logw - m_t)
    w_inter = jnp.exp(inter - m_t)
    den = (jnp.sum(s, axis=1, keepdims=True)
           + w_inter * jnp.sum(q.astype(_F32) * n_sc[...], axis=1, keepdims=True))
    inv = 1.0 / jnp.maximum(jnp.abs(den), jnp.exp(-m_t))
    sb = s.astype(_BF16)
    kw = k.astype(_F32) * wg_col
    kwb = kw.astype(_BF16)
    n_sc[...] = decay * n_sc[...] + jnp.sum(kw, axis=0, keepdims=True)
    m_sc[...] = m_new
    dv = v.shape[1]
    step = min(dv, 2 * LANES)
    pieces = []
    for c0 in range(0, dv, step):
        cols = slice(c0, c0 + step)
        vc = v[:, cols]
        num = (jnp.dot(sb, vc, preferred_element_type=_F32)
               + w_inter * jnp.dot(q, cb_sc[:, cols], preferred_element_type=_F32))
        pieces.append(num * inv)
        c_new = decay * c_sc[:, cols] + lax.dot_general(kwb, vc, (((0,), (0,)), ((), ())),
                                                        preferred_element_type=_F32)
        c_sc[:, cols] = c_new
        cb_sc[:, cols] = c_new.astype(_BF16)
    return jnp.concatenate(pieces, axis=1)


def _mlstm_scan(q, k, v, gates, uz, xc, hnorm_w, skip, seq, nh):
    bsz, t_all, e = q.shape
    dh = e // nh
    n_ctx = t_all - seq
    ln = _tile(math.gcd(seq, n_ctx), SCAN_CHUNK, 8)
    nc = t_all // ln
    ncl = seq // ln
    ncc = n_ctx // ln
    g = gates.reshape(bsz, nc, ln, 2, 2, nh).transpose(3, 4, 0, 5, 1, 2).reshape(2, 2, bsz, nh, nc, 1, ln)
    act = jax.ShapeDtypeStruct((bsz, t_all, e), _BF16)

    def run(rev, extra_in, extra_specs):
        hb = SCAN_HEADS_PER_STEP[int(rev)]
        hb = hb if nh % hb == 0 else 1
        scratch = [pltpu.VMEM((hb, dh, dh), _F32), pltpu.VMEM((hb, dh, dh), _BF16),
                   pltpu.VMEM((hb, 1, dh), _F32), pltpu.VMEM((hb, 1, 1), _F32)]

        def chunk(s):
            if rev:
                return jnp.where(s < ncc, ncl + ncc - 1 - s, ncl - 1 - (s - ncc))
            return jnp.where(s < ncc, ncl + s, s - ncc)

        tok = lambda col0: pl.BlockSpec((None, ln, hb * dh), lambda b, hh, s: (b, chunk(s), col0 + hh))
        gate = pl.BlockSpec((None, hb, None, 1, ln), lambda b, hh, s: (b, hh, chunk(s), 0, 0))
        return pl.pallas_call(
            functools.partial(_scan_kernel, rev=rev),
            grid=(bsz, nh // hb, nc),
            in_specs=[tok(0), tok(0), tok(0), gate, gate, *extra_specs(tok, hb)],
            out_specs=tok(0),
            out_shape=act,
            scratch_shapes=scratch,
            compiler_params=_cparams("parallel", "parallel", "arbitrary"),
            name="mlstm_scan_rev" if rev else "mlstm_scan_fwd",
        )(q, k, v, g[0, int(rev)], g[1, int(rev)], *extra_in)

    h_fwd = run(False, (), lambda tok, hb: ())
    vec = lambda hb: pl.BlockSpec((1, hb * dh), lambda b, hh, s: (0, hh))
    return run(True, (h_fwd, xc, uz, hnorm_w.reshape(1, e), skip.reshape(1, e)),
               lambda tok, hb: (tok(0), tok(0), tok(nh // hb), vec(hb), vec(hb)))


def _mlstm_branch(uz, conv_w, conv_b, wq, wk, wv, w_ig, b_ig, w_fg, b_fg, hnorm_w, skip, seq):
    nh = w_ig.shape[-1]
    xc, q, k, v, gates = _mlstm_pre(uz, conv_w, conv_b, wq, wk, wv, w_ig, b_ig, w_fg, b_fg, seq)
    return _mlstm_scan(q, k, v, gates, uz, xc, hnorm_w, skip, seq, nh)


def _pool_kernel(u_ref, o_ref, *, seq, n_ctx, tiles_per_group):
    grp = pl.program_id(1) // tiles_per_group
    for gi, w in enumerate(POOL_WINDOWS):
        lo = w // 2
        hi = w - 1 - lo
        assert hi == lo - 1 and lo <= EDGE_ROWS

        def delta(u, bounds, w=w, lo=lo, hi=hi):
            trail = u
            lead = u
            span = 1
            while span < lo:
                trail = trail + _shift_rows(trail, -span, bounds)
                if bounds is not None:
                    lead = lead + _shift_rows(lead, span, bounds)
                span *= 2
            if bounds is None:
                lead = _shift_rows(trail, lo - 1, None) if lo > 1 else u
                return (_shift_rows(trail, -1, None) + lead) * (1.0 / w) - u
            win = _shift_rows(trail, -1, bounds) + lead
            pos, seg_len = bounds
            cnt = jnp.minimum(pos + hi + 1, seg_len) - jnp.maximum(pos - lo, 0)
            return win / cnt.astype(_F32) - u

        @pl.when(grp == gi)
        def _(delta=delta):
            o_ref[...] = delta(u_ref[...].astype(_F32), None).astype(o_ref.dtype)
            _redo_edges(delta, u_ref, o_ref, seq, seq + n_ctx)


def _pool_group_linear(a, w_grp, uz, scale):
    bsz, t_all, e = a.shape
    ng, gw, _ = w_grp.shape
    tm = _tile(t_all, 1088, 16)
    tn = _tile(gw, 1024, LANES)
    npg = gw // tn

    def epilogue(acc, pids, z_ref, sc_ref):
        return acc * sc_ref[...] * _silu(z_ref[...].astype(_F32))

    extras = [uz, scale.reshape(1, e)]
    especs = [pl.BlockSpec((None, tm, tn), lambda b, g, j, i, k: (b, i, (e // tn) + g * npg + j)),
              pl.BlockSpec((1, tn), lambda b, g, j, i, k: (0, g * npg + j))]
    return _matmul(
        (bsz, ng, npg, t_all // tm, 1),
        a, pl.BlockSpec((None, tm, gw), lambda b, g, j, i, k: (b, i, g)),
        w_grp, pl.BlockSpec((None, gw, tn), lambda b, g, j, i, k: (g, 0, j)),
        jax.ShapeDtypeStruct((bsz, t_all, e), _BF16),
        pl.BlockSpec((None, tm, tn), lambda b, g, j, i, k: (b, i, g * npg + j)),
        extras=tuple(extras), extra_specs=tuple(especs), epilogue=epilogue, name="pool_group")


def _pool_branch(uz, w_grp, scale, seq):
    bsz, t_all, e2 = uz.shape
    e = e2 // 2
    te = _tile(e // N_GROUPS, 256, LANES)
    tok = pl.BlockSpec((None, t_all, te), lambda b, j: (b, 0, j))
    dlt = pl.pallas_call(
        functools.partial(_pool_kernel, seq=seq, n_ctx=t_all - seq, tiles_per_group=e // N_GROUPS // te),
        grid=(bsz, e // te),
        in_specs=[tok],
        out_specs=tok,
        out_shape=jax.ShapeDtypeStruct((bsz, t_all, e), _BF16),
        compiler_params=_cparams("parallel", "parallel"),
        name="pool",
    )(uz)
    return _pool_group_linear(dlt, w_grp, uz, scale)


def _cos_sin(n_out, n_in, period):
    ang = (np.outer(np.arange(n_out), np.arange(n_in)) % period) * (2.0 * np.pi / period)
    return np.cos(ang), np.sin(ang)


@functools.lru_cache(maxsize=None)
def _chan_table(gw):
    c, s = _cos_sin(gw, gw, gw)
    return np.concatenate([c, s], axis=0) * gw ** -0.5


@functools.lru_cache(maxsize=None)
def _time_tables(seq, n_ctx):
    n1 = max(f for f in range(1, math.isqrt(seq) + 1) if seq % f == 0)
    n2 = seq // n1
    g = SUBLANES
    assert n1 % g == 0 and n2 % g == 0
    eye = np.eye(g)
    c1, s1 = _cos_sin(n1, n1, n1)
    base = np.block([[c1, -s1], [-s1, -c1]]) * seq ** -0.5
    stage1 = np.einsum('ab,rkst->arkstb', eye, base.reshape(2, n1, 2, n1)).reshape(g * 2 * n1, 2 * n1 * g)
    twc, tws = _cos_sin(n2, n1, seq)
    c2, s2 = _cos_sin(n2, n2, n2)
    cs2 = np.concatenate([c2, s2], axis=1)
    stage2 = np.einsum('ab,krt->kartb', eye, cs2.reshape(n2, 2, n2)).reshape(n2 * g, 2 * n2 * g)
    cc, sc = _cos_sin(n_ctx, n_ctx, n_ctx)
    ctx_tab = np.concatenate([cc, -sc], axis=1) * n_ctx ** -0.5
    lanes = lambda a: np.repeat(a[:, :, None], LANES, axis=2)
    return n1, n2, stage1, lanes(twc), lanes(tws), stage2, ctx_tab


def _time_dft_kernel(y1_ref, y2_ref, z_ref, w1_ref, twc_ref, tws_ref, w2_ref, wc_ref, o_ref,
                     y1f, y2f, ar, ai, out, *, seq, n1, n2):
    g = SUBLANES
    tc = o_ref.shape[-1]
    wide = lambda tab: jnp.concatenate([tab] * (tc // LANES), axis=1)
    y1f[...] = y1_ref[:seq, :].astype(_F32).reshape(y1f.shape)
    y2f[...] = y2_ref[:seq, :].astype(_F32).reshape(y2f.shape)
    for th in range(n2 // g):
        rhs = jnp.concatenate([y1f[:, th].reshape(n1 * g, tc), y2f[:, th].reshape(n1 * g, tc)], axis=0)
        res = jnp.dot(w1_ref[...], rhs.astype(_BF16), preferred_element_type=_F32)
        for tl in range(g):
            t2 = th * g + tl
            a_r = res[tl * 2 * n1:tl * 2 * n1 + n1]
            a_i = res[tl * 2 * n1 + n1:(tl + 1) * 2 * n1]
            c, s = wide(twc_ref[t2]), wide(tws_ref[t2])
            ar[t2] = (a_r * c + a_i * s).reshape(ar.shape[1:])
            ai[t2] = (a_i * c - a_r * s).reshape(ai.shape[1:])
    for kh in range(n1 // g):
        rhs = jnp.concatenate([ar[:, kh].reshape(n2 * g, tc), ai[:, kh].reshape(n2 * g, tc)], axis=0)
        res = jnp.dot(w2_ref[...], rhs.astype(_BF16), preferred_element_type=_F32)
        out[:, kh] = res.reshape(n2, g, tc)
    gate = _silu(z_ref[:seq, :].astype(_F32))
    o_ref[:seq, :] = (out[...].reshape(seq, tc) * gate).astype(o_ref.dtype)
    rhs = jnp.concatenate([y1_ref[seq:, :], y2_ref[seq:, :]], axis=0)
    res = jnp.dot(wc_ref[...], rhs, preferred_element_type=_F32)
    o_ref[seq:, :] = (res * _silu(z_ref[seq:, :].astype(_F32))).astype(o_ref.dtype)


def _fourier_branch(uz, w_grp, seq):
    bsz, t_all, e2 = uz.shape
    n_ctx = t_all - seq
    e = e2 // 2
    gw = e // N_GROUPS
    tab = jnp.asarray(_chan_table(gw), dtype=_BF16)
    tw = _tile(gw, 1024, LANES)
    nrow = gw // tw
    wcomb = _matmul(
        (N_GROUPS, gw // tw, 2 * nrow, 1),
        tab, pl.BlockSpec((tw, gw), lambda g, j, i, k: (i, 0)),
        w_grp, pl.BlockSpec((None, gw, tw), lambda g, j, i, k: (g, 0, j)),
        jax.ShapeDtypeStruct((N_GROUPS, gw, 2 * gw), _BF16),
        pl.BlockSpec((None, tw, tw), lambda g, j, i, k: (g, i % nrow, (i // nrow) * (gw // tw) + j)),
        name="fourier_weight")
    tm = _tile(t_all, 1088, 16)
    tn = _tile(gw, 1024, LANES)
    y = _matmul(
        (bsz, N_GROUPS, 2 * gw // tn, t_all // tm, 1),
        uz, pl.BlockSpec((None, tm, gw), lambda b, g, j, i, k: (b, i, g)),
        wcomb, pl.BlockSpec((None, gw, tn), lambda b, g, j, i, k: (g, 0, j)),
        jax.ShapeDtypeStruct((bsz, t_all, 2 * e), _BF16),
        pl.BlockSpec((None, tm, tn), lambda b, g, j, i, k: (b, i, g * (2 * gw // tn) + j)),
        name="dft_chan")
    n1, n2, stage1, twc, tws, stage2, ctx_tab = _time_tables(seq, n_ctx)
    tc = _tile(gw, 256, LANES)
    cpg = gw // tc

    def ycol(part):
        return lambda b, c: (b, 0, (2 * (c // cpg) + part) * cpg + c % cpg)

    tok = lambda imap: pl.BlockSpec((None, t_all, tc), imap)
    full = lambda a: pl.BlockSpec(a.shape, lambda b, c: (0,) * a.ndim, pipeline_mode=pl.Buffered(1))
    tabs = [jnp.asarray(stage1, dtype=_BF16), jnp.asarray(twc, dtype=_F32), jnp.asarray(tws, dtype=_F32),
            jnp.asarray(stage2, dtype=_BF16), jnp.asarray(ctx_tab, dtype=_BF16)]
    by_t1 = pltpu.VMEM((n1, n2 // SUBLANES, SUBLANES, tc), _F32)
    by_t2 = pltpu.VMEM((n2, n1 // SUBLANES, SUBLANES, tc), _F32)
    return pl.pallas_call(
        functools.partial(_time_dft_kernel, seq=seq, n1=n1, n2=n2),
        grid=(bsz, e // tc),
        in_specs=[tok(ycol(0)), tok(ycol(1)), tok(lambda b, c: (b, 0, e // tc + c))] + [full(t) for t in tabs],
        out_specs=tok(lambda b, c: (b, 0, c)),
        out_shape=jax.ShapeDtypeStruct((bsz, t_all, e), _BF16),
        scratch_shapes=[by_t1, by_t1, by_t2, by_t2, by_t2],
        compiler_params=_cparams("parallel", "parallel"),
        name="dft_time",
    )(y, y, uz, *tabs)


def kernel(x, c, ctx, c_ctx, ada_w, ada_b, norm_g, final_g, a_w_in, a_conv_w, a_conv_b, a_wq, a_wk, a_wv,
           a_w_ig, a_b_ig, a_w_fg, a_b_fg, a_hnorm_w, a_skip, a_w_out, b_w_in, b_w_grp, b_scale, b_w_out,
           c_w_in, c_w_grp, c_w_out):
    bsz, seq, d = x.shape
    depth = ada_w.shape[0]
    n_rows = 8
    assert bsz + 1 <= n_rows
    rows = jnp.concatenate([c, c_ctx[None], jnp.zeros((n_rows - bsz - 1, d), c.dtype)], axis=0)
    ada_b3 = ada_b.reshape(depth, 1, 3 * d)
    mods = _ada(rows, ada_w, ada_b, 1).reshape(n_rows, 1, 3 * d)
    xs = _embed(x, ctx)
    for i in range(depth):
        kind = i % N_MIXERS
        j = i // N_MIXERS
        h = _prenorm(xs, norm_g[i], mods, seq)
        w_in, w_out = ((a_w_in, a_w_out), (b_w_in, b_w_out), (c_w_in, c_w_out))[kind]
        if i + 1 < depth:
            uz, mods_next = _in_proj(h, w_in, j, side=(rows, ada_w, ada_b3, i + 1))
        else:
            uz, mods_next = _in_proj(h, w_in, j), None
        if kind == 0:
            a = _mlstm_branch(uz, a_conv_w[j], a_conv_b[j], a_wq[j], a_wk[j], a_wv[j],
                              a_w_ig[j], a_b_ig[j], a_w_fg[j], a_b_fg[j], a_hnorm_w[j], a_skip[j], seq)
        elif kind == 1:
            a = _pool_branch(uz, b_w_grp[j], b_scale[j], seq)
        else:
            a = _fourier_branch(uz, c_w_grp[j], seq)
        xs = _out_proj(a, w_out, j, xs, mods, seq)
        if mods_next is not None:
            mods = mods_next.reshape(n_rows, 1, 3 * d)
    return _final_norm(xs, final_g, seq)
```

```python
import functools
import math

import jax
import jax.numpy as jnp
import numpy as np
from jax import lax
from jax.experimental import pallas as pl
from jax.experimental.pallas import tpu as pltpu

GRID_W = 64
N_MIXERS = 3
QKV_BLOCK = 4
CONV_W = 4
POOL_WINDOWS = (2, 4, 8, 16)
N_GROUPS = 4
POS_BASE = 10000.0
EPS = 1e-6

LANES = 128
SUBLANES = 8
VMEM_LIMIT_BYTES = 56 * 1024 * 1024
SCAN_CHUNK = 256
SCAN_HEADS_PER_STEP = (4, 2)

_F32 = jnp.float32
_BF16 = jnp.bfloat16


def _cparams(*sem):
    return pltpu.CompilerParams(dimension_semantics=sem, vmem_limit_bytes=VMEM_LIMIT_BYTES)


def _tile(n, target, align):
    best = None
    for t in range(align, min(n, target) + 1, align):
        if n % t == 0:
            best = t
    return n if best is None else best


def _silu(v):
    half = 0.5 * v
    return half + half * jnp.tanh(half)


def _matmul(grid, a, a_spec, b, b_spec, out_shape, out_spec, extras=(), extra_specs=(), epilogue=None,
            name="matmul"):
    n_extra = len(extras)
    if epilogue is None:
        epilogue = lambda acc, pids: acc

    def body(*refs):
        a_ref, b_ref = refs[0], refs[1]
        ex = refs[2:2 + n_extra]
        o_ref = refs[2 + n_extra]
        pids = tuple(pl.program_id(ax) for ax in range(len(grid)))
        acc = jnp.dot(a_ref[...].astype(_BF16), b_ref[...].astype(_BF16), preferred_element_type=_F32)
        o_ref[...] = epilogue(acc, pids, *ex).astype(o_ref.dtype)

    return pl.pallas_call(
        body,
        grid=grid,
        in_specs=[a_spec, b_spec, *extra_specs],
        out_specs=out_spec,
        out_shape=out_shape,
        compiler_params=_cparams(*(("parallel",) * len(grid))),
        name=name,
    )(a, b, *extras)


def _proj(a, w, layer, tm, tn, out_dtype, name, extras=(), extra_blocks=(), epilogue=None, side=None):
    bsz, t_all, k = a.shape
    n = w.shape[2]
    nt, mt = n // tn, t_all // tm
    pieces = bsz * mt
    tkp = k // pieces
    assert tkp * pieces == k and tkp % 16 == 0
    n_extra = len(extras)
    if epilogue is None:
        epilogue = lambda acc, pids: acc
    if side is not None:
        s_rows, s_w, s_b, s_layer = side
        tns = s_w.shape[2] // nt
        assert s_w.shape[1] == k and tns * nt == s_w.shape[2] and tns % LANES == 0

    def body(*refs):
        a_ref, w_ref = refs[0], refs[1]
        ex = refs[2:2 + n_extra]
        rest = refs[2 + n_extra:]
        if side is not None:
            sr_ref, sw_ref, sb_ref, o_ref, so_ref, w_bf = rest
        else:
            o_ref, w_bf = rest
        pids = tuple(pl.program_id(ax) for ax in range(3))
        jj = pids[0]
        piece = pids[1] * mt + pids[2]

        def stage():
            rows = pl.ds(pl.multiple_of(piece * tkp, tkp), tkp)
            w_bf[jj % 2, rows, :] = w_ref[...].astype(_BF16)

        @pl.when(jj == 0)
        def _():
            stage()

        @pl.when(jj > 0)
        def _():
            stage()
            acc = jnp.dot(a_ref[...], w_bf[(jj + 1) % 2], preferred_element_type=_F32)
            o_ref[...] = epilogue(acc, pids, *ex).astype(o_ref.dtype)
            if side is not None:
                part = jnp.dot(_silu(sr_ref[...]).astype(_BF16), sw_ref[...].astype(_BF16),
                               preferred_element_type=_F32)

                @pl.when(piece == 0)
                def _():
                    so_ref[...] = part + sb_ref[...]

                @pl.when(piece > 0)
                def _():
                    so_ref[...] += part

    def a_map(jj, b, i):
        return (jnp.where(jj > 0, b, 0), jnp.where(jj > 0, i, 0), 0)

    def o_map(jj, b, i):
        return (jnp.where(jj > 0, b, 0), jnp.where(jj > 0, i, 0), jnp.maximum(jj - 1, 0))

    def w_map(jj, b, i):
        last = jj == nt
        return (layer, jnp.where(last, pieces - 1, b * mt + i), jnp.minimum(jj, nt - 1))

    in_specs = [pl.BlockSpec((None, tm, k), a_map), pl.BlockSpec((None, tkp, tn), w_map),
                *[spec(o_map) for spec in extra_blocks]]
    out_specs = [pl.BlockSpec((None, tm, tn), o_map)]
    out_shape = [jax.ShapeDtypeStruct((bsz, t_all, n), out_dtype)]
    operands = [a, w, *extras]
    if side is not None:
        nr = s_rows.shape[0]
        tile_s = lambda jj: jnp.maximum(jj - 1, 0)
        piece_s = lambda jj, b, i: jnp.where(jj > 0, b * mt + i, 0)
        in_specs += [pl.BlockSpec((nr, tkp), lambda jj, b, i: (0, piece_s(jj, b, i))),
                     pl.BlockSpec((None, tkp, tns), lambda jj, b, i: (s_layer, piece_s(jj, b, i), tile_s(jj))),
                     pl.BlockSpec((None, 1, tns), lambda jj, b, i: (s_layer, 0, tile_s(jj)))]
        out_specs.append(pl.BlockSpec((nr, tns), lambda jj, b, i: (0, tile_s(jj))))
        out_shape.append(jax.ShapeDtypeStruct((nr, s_w.shape[2]), _F32))
        operands += [s_rows, s_w, s_b]
    outs = pl.pallas_call(
        body,
        grid=(nt + 1, bsz, mt),
        in_specs=in_specs,
        out_specs=out_specs,
        out_shape=out_shape,
        scratch_shapes=[pltpu.VMEM((2, k, tn), _BF16)],
        compiler_params=_cparams("arbitrary", "arbitrary", "arbitrary"),
        name=name,
    )(*operands)
    return outs[0] if side is None else outs


def _ada_kernel(s_ref, w_ref, b_ref, o_ref):
    s = _silu(s_ref[...]).astype(_BF16)
    acc = jnp.dot(s, w_ref[...].astype(_BF16), preferred_element_type=_F32)
    o_ref[...] = acc + b_ref[...]


def _ada(rows, ada_w, ada_b, depth):
    _, d, n3 = ada_w.shape
    nr = rows.shape[0]
    tn = _tile(n3, 1024, LANES)
    return pl.pallas_call(
        _ada_kernel,
        grid=(depth, n3 // tn),
        in_specs=[
            pl.BlockSpec((nr, d), lambda i, n: (0, 0)),
            pl.BlockSpec((None, d, tn), lambda i, n: (i, 0, n)),
            pl.BlockSpec((None, 1, tn), lambda i, n: (i, 0, n)),
        ],
        out_specs=pl.BlockSpec((None, nr, tn), lambda i, n: (i, 0, n)),
        out_shape=jax.ShapeDtypeStruct((depth, nr, n3), _F32),
        compiler_params=_cparams("parallel", "parallel"),
        name="ada",
    )(rows, ada_w, ada_b.reshape(-1, 1, n3))


def _embed_kernel(x_ref, c_ref, re_ref, ce_ref, o_ref, *, n_lat, half):
    r = pl.program_id(1)

    @pl.when(r < n_lat)
    def _():
        for i in range(re_ref.shape[0]):
            rows = slice(i * GRID_W, (i + 1) * GRID_W)
            o_ref[rows, :half] = x_ref[rows, :half] + re_ref[i:i + 1, :]
            o_ref[rows, half:] = x_ref[rows, half:] + ce_ref[...]

    @pl.when(r >= n_lat)
    def _():
        o_ref[...] = c_ref[...]


def _embed(x, ctx):
    bsz, seq, d = x.shape
    n_ctx = ctx.shape[1]
    n_rows = seq // GRID_W
    rb = max(f for f in (4, 2, 1) if n_rows % f == 0 and n_ctx % (f * GRID_W) == 0)
    tb = rb * GRID_W
    n_lat = seq // tb
    half = d // 2
    quarter = d // 4
    omega = 1.0 / (POS_BASE ** (jnp.arange(quarter, dtype=_F32) / quarter))

    def axis_emb(p):
        ang = p[:, None] * omega[None, :]
        return jnp.concatenate([jnp.sin(ang), jnp.cos(ang)], axis=-1)

    row_emb = axis_emb(jnp.arange(n_rows, dtype=_F32)).reshape(n_lat, rb, half)
    col_emb = axis_emb(jnp.arange(GRID_W, dtype=_F32))
    return pl.pallas_call(
        functools.partial(_embed_kernel, n_lat=n_lat, half=half),
        grid=(bsz, n_lat + n_ctx // tb),
        in_specs=[
            pl.BlockSpec((None, tb, d), lambda b, r: (b, jnp.minimum(r, n_lat - 1), 0)),
            pl.BlockSpec((None, tb, d), lambda b, r: (b, jnp.maximum(r - n_lat, 0), 0)),
            pl.BlockSpec((None, rb, half), lambda b, r: (jnp.minimum(r, n_lat - 1), 0, 0)),
            pl.BlockSpec((GRID_W, half), lambda b, r: (0, 0)),
        ],
        out_specs=pl.BlockSpec((None, tb, d), lambda b, r: (b, r, 0)),
        out_shape=jax.ShapeDtypeStruct((bsz, seq + n_ctx, d), x.dtype),
        compiler_params=_cparams("parallel", "parallel"),
        name="embed",
    )(x, ctx, row_emb, col_emb)


def _prenorm_kernel(x_ref, g_ref, sh_ref, sc_ref, o_ref):
    x = x_ref[...]
    y = x * lax.rsqrt(jnp.mean(x * x, axis=-1, keepdims=True) + EPS)
    o_ref[...] = (y * g_ref[...] * (1.0 + sc_ref[...]) + sh_ref[...]).astype(o_ref.dtype)


def _prenorm(xs, g, mods, seq):
    bsz, t_all, d = xs.shape
    tb = _tile(math.gcd(seq, t_all - seq), 256, 8)
    n_lat = seq // tb

    def mrow(b, j):
        return jnp.where(j < n_lat, b, bsz)

    return pl.pallas_call(
        _prenorm_kernel,
        grid=(bsz, t_all // tb),
        in_specs=[
            pl.BlockSpec((None, tb, d), lambda b, j: (b, j, 0)),
            pl.BlockSpec((1, d), lambda b, j: (0, 0)),
            pl.BlockSpec((None, 1, d), lambda b, j: (mrow(b, j), 0, 0)),
            pl.BlockSpec((None, 1, d), lambda b, j: (mrow(b, j), 0, 1)),
        ],
        out_specs=pl.BlockSpec((None, tb, d), lambda b, j: (b, j, 0)),
        out_shape=jax.ShapeDtypeStruct((bsz, t_all, d), _BF16),
        compiler_params=_cparams("parallel", "parallel"),
        name="prenorm",
    )(xs, g.reshape(1, d), mods, mods)


def _final_norm_kernel(x_ref, g_ref, o_ref):
    x = x_ref[...]
    y = x * lax.rsqrt(jnp.mean(x * x, axis=-1, keepdims=True) + EPS)
    o_ref[...] = y * g_ref[...]


def _final_norm(xs, g, seq):
    bsz, _, d = xs.shape
    tb = _tile(seq, 512, 8)
    return pl.pallas_call(
        _final_norm_kernel,
        grid=(bsz, seq // tb),
        in_specs=[
            pl.BlockSpec((None, tb, d), lambda b, j: (b, j, 0)),
            pl.BlockSpec((1, d), lambda b, j: (0, 0)),
        ],
        out_specs=pl.BlockSpec((None, tb, d), lambda b, j: (b, j, 0)),
        out_shape=jax.ShapeDtypeStruct((bsz, seq, d), xs.dtype),
        compiler_params=_cparams("parallel", "parallel"),
        name="final_norm",
    )(xs, g.reshape(1, d))


def _in_proj(h, w, layer, side=None):
    t_all, n = h.shape[1], w.shape[2]
    return _proj(h, w, layer, _tile(t_all, 1088, 16), _tile(n, 1024, LANES), _BF16, "in_proj", side=side)


def _out_proj(a, w, layer, xs, mods, seq):
    bsz, t_all, _ = a.shape
    d = w.shape[2]
    tm = _tile(t_all, 544, 16)
    tn = _tile(d, 512, LANES)
    g_off = 2 * (d // tn)

    def epilogue(acc, pids, res_ref, gl_ref, gc_ref):
        row = pids[2] * tm + lax.broadcasted_iota(jnp.int32, acc.shape, 0)
        gate = jnp.where(row < seq, gl_ref[...], gc_ref[...])
        return res_ref[...] + gate * acc

    def gate_block(row_of):
        return lambda om: pl.BlockSpec((None, 1, tn), lambda jj, b, i: (row_of(b), 0, g_off + om(jj, b, i)[2]))

    return _proj(a, w, layer, tm, tn, xs.dtype, "out_proj", extras=(xs, mods, mods),
                 extra_blocks=(lambda om: pl.BlockSpec((None, tm, tn), om),
                               gate_block(lambda b: b), gate_block(lambda b: bsz)),
                 epilogue=epilogue)


EDGE_ROWS = 16
EDGE_HALO = 16


def _segment_pos(shape, seq, n_ctx, row0):
    t = row0 + lax.broadcasted_iota(jnp.int32, shape, 0)
    is_lat = t < seq
    return jnp.where(is_lat, t, t - seq), jnp.where(is_lat, seq, n_ctx)


def _shift_rows(u, k, bounds=None):
    rolled = pltpu.roll(u, (-k) % u.shape[0], axis=0)
    if bounds is None:
        return rolled
    pos, seg_len = bounds
    valid = (pos >= -k) if k < 0 else (pos < seg_len - k)
    return jnp.where(valid, rolled, 0.0)


def _edge_windows(seq, t_all):
    p, h = EDGE_ROWS, EDGE_HALO
    assert seq >= p + h and t_all - seq >= 2 * (p + h)
    return ((0, p + h, 0, p), (seq - p - h, 2 * (p + h), h, 2 * p), (t_all - p - h, p + h, h, p))


def _redo_edges(fn, u_ref, dst_ref, seq, t_all):
    for row0, rows, off, n in _edge_windows(seq, t_all):
        uw = u_ref[row0:row0 + rows, :].astype(_F32)
        res = fn(uw, _segment_pos(uw.shape, seq, t_all - seq, row0))
        dst_ref[row0 + off:row0 + off + n, :] = res[off:off + n].astype(dst_ref.dtype)


def _mlstm_pre_kernel(u_ref, cw_ref, cb_ref, wq_ref, wk_ref, wv_ref, wg_ref, gb_ref,
                      xc_ref, q_ref, k_ref, v_ref, g_ref, acc_sc, *, seq, n_ctx, k_scale, n_gate):
    j = pl.program_id(1)

    @pl.when(j == 0)
    def _():
        g_ref[...] = jnp.broadcast_to(gb_ref[...], g_ref.shape)

    ub = u_ref[...]
    left = (CONV_W - 1) // 2

    def conv(u, bounds):
        acc = cb_ref[...] + u * cw_ref[left:left + 1, :]
        for jj in range(CONV_W):
            if jj != left:
                acc = acc + _shift_rows(u, jj - left, bounds) * cw_ref[jj:jj + 1, :]
        return acc

    acc_sc[...] = conv(ub.astype(_F32), None)
    _redo_edges(conv, u_ref, acc_sc, seq, seq + n_ctx)
    xc = _silu(acc_sc[...])
    xcb = xc.astype(_BF16)
    te = ub.shape[1]
    own_block = (lax.broadcasted_iota(jnp.int32, (te, te), 0) // QKV_BLOCK
                 == lax.broadcasted_iota(jnp.int32, (te, te), 1) // QKV_BLOCK)
    dense = lambda w_ref: jnp.where(own_block, w_ref[...].astype(_F32), 0.0).astype(_BF16)
    q = jnp.dot(xcb, dense(wq_ref), preferred_element_type=_F32)
    k = jnp.dot(xcb, dense(wk_ref), preferred_element_type=_F32)
    v = jnp.dot(ub, dense(wv_ref), preferred_element_type=_F32)
    qb, kb, vb = q.astype(_BF16), k.astype(_BF16), v.astype(_BF16)
    xc_ref[...] = xcb
    q_ref[...] = qb
    k_ref[...] = (k * k_scale).astype(_BF16)
    v_ref[...] = vb
    g_ref[...] += jnp.dot(jnp.concatenate([qb, kb, vb], axis=1), wg_ref[...], preferred_element_type=_F32)

    @pl.when(j == pl.num_programs(1) - 1)
    def _():
        g = g_ref[...]
        col = lax.broadcasted_iota(jnp.int32, g.shape, 1)
        log_sig = jnp.minimum(g, 0.0) - jnp.log1p(jnp.exp(-jnp.abs(g)))
        g_ref[...] = jnp.where(col < n_gate, g, log_sig)


def _blockdiag_rows(w, te):
    rows = w.reshape(w.shape[0] * QKV_BLOCK, QKV_BLOCK).astype(_BF16)
    return jnp.tile(rows, (1, te // QKV_BLOCK))


def _mlstm_pre(uz, conv_w, conv_b, wq, wk, wv, w_ig, b_ig, w_fg, b_fg, seq):
    bsz, t_all, e2 = uz.shape
    e = e2 // 2
    nh = w_ig.shape[-1]
    te = _tile(e, 256, LANES)
    nt = e // te
    ng = 4 * nh
    wg = jnp.concatenate([w_ig[0], w_ig[1], w_fg[0], w_fg[1]], axis=-1)
    wg = wg.reshape(3, nt, te, ng).transpose(1, 0, 2, 3).reshape(nt, 3 * te, ng).astype(_BF16)
    gb = jnp.concatenate([b_ig[0], b_ig[1], b_fg[0], b_fg[1]], axis=-1).reshape(1, ng)
    tok = pl.BlockSpec((None, t_all, te), lambda b, j: (b, 0, j))
    bd = pl.BlockSpec((te, te), lambda b, j: (j, 0))
    act = jax.ShapeDtypeStruct((bsz, t_all, e), _BF16)
    return pl.pallas_call(
        functools.partial(_mlstm_pre_kernel, seq=seq, n_ctx=t_all - seq,
                          k_scale=float((e // nh) ** -0.5), n_gate=2 * nh),
        grid=(bsz, nt),
        in_specs=[
            tok,
            pl.BlockSpec((CONV_W, te), lambda b, j: (0, j)),
            pl.BlockSpec((1, te), lambda b, j: (0, j)),
            bd, bd, bd,
            pl.BlockSpec((None, 3 * te, ng), lambda b, j: (j, 0, 0)),
            pl.BlockSpec((1, ng), lambda b, j: (0, 0)),
        ],
        out_specs=[tok, tok, tok, tok, pl.BlockSpec((None, t_all, ng), lambda b, j: (b, 0, 0))],
        out_shape=[act, act, act, act, jax.ShapeDtypeStruct((bsz, t_all, ng), _F32)],
        scratch_shapes=[pltpu.VMEM((t_all, te), _F32)],
        compiler_params=_cparams("parallel", "arbitrary"),
        name="mlstm_pre",
    )(uz, conv_w, conv_b.reshape(1, e), _blockdiag_rows(wq, te), _blockdiag_rows(wk, te),
      _blockdiag_rows(wv, te), wg, gb)


def _scan_kernel(q_ref, k_ref, v_ref, li_ref, lf_ref, *rest, rev):
    if rev:
        hf_ref, xc_ref, z_ref, nw_ref, sk_ref, o_ref, c_sc, cb_sc, n_sc, m_sc = rest
    else:
        o_ref, c_sc, cb_sc, n_sc, m_sc = rest

    @pl.when(pl.program_id(2) == 0)
    def _():
        c_sc[...] = jnp.zeros(c_sc.shape, _F32)
        cb_sc[...] = jnp.zeros(cb_sc.shape, _BF16)
        n_sc[...] = jnp.zeros(n_sc.shape, _F32)
        m_sc[...] = jnp.zeros(m_sc.shape, _F32)

    ln = q_ref.shape[0]
    dh = c_sc.shape[-1]
    ti = lax.broadcasted_iota(jnp.int32, (ln, ln), 0)
    si = lax.broadcasted_iota(jnp.int32, (ln, ln), 1)
    incl = si >= ti if rev else si <= ti
    incl_t = ti >= si if rev else ti <= si
    eye = si == ti
    for hh in range(c_sc.shape[0]):
        cols = slice(hh * dh, (hh + 1) * dh)
        h = _scan_chunk(q_ref[:, cols], k_ref[:, cols], v_ref[:, cols], li_ref[hh], lf_ref[hh],
                        incl, incl_t, eye, c_sc.at[hh], cb_sc.at[hh], n_sc.at[hh], m_sc.at[hh])
        if rev:
            h = h + hf_ref[:, cols].astype(_F32)
            mu = jnp.mean(h, axis=-1, keepdims=True)
            hc = h - mu
            var = jnp.mean(hc * hc, axis=-1, keepdims=True)
            hn = hc * lax.rsqrt(var + EPS) * nw_ref[:, cols]
            h = (hn + sk_ref[:, cols] * xc_ref[:, cols].astype(_F32)) * _silu(z_ref[:, cols].astype(_F32))
        o_ref[:, cols] = h.astype(o_ref.dtype)


def _scan_chunk(q, k, v, li_r, lf_r, incl, incl_t, eye, c_sc, cb_sc, n_sc, m_sc):
    ln = q.shape[0]
    lf_b = jnp.broadcast_to(lf_r, (ln, ln))
    li_b = jnp.broadcast_to(li_r, (ln, ln))
    b_col = jnp.sum(jnp.where(incl, lf_b, 0.0), axis=1, keepdims=True)
    lf_col = jnp.sum(jnp.where(eye, lf_b, 0.0), axis=1, keepdims=True)
    li_col = jnp.sum(jnp.where(eye, li_b, 0.0), axis=1, keepdims=True)
    b_row = jnp.sum(jnp.where(incl_t, lf_col, 0.0), axis=0, keepdims=True)
    b_end = jnp.sum(lf_r, axis=1, keepdims=True)
    m_old = m_sc[...]
    g_row = b_end - b_row + li_r
    g_col = b_end - b_col + li_col
    m_new = jnp.maximum(b_end + m_old, jnp.max(g_row, axis=1, keepdims=True))
    wg_col = jnp.exp(g_col - m_new)
    decay = jnp.exp(b_end + m_old - m_new)

    logw = jnp.where(incl, b_col - b_row + li_r, -jnp.inf)
    inter = b_col + m_old
    m_t = jnp.maximum(jnp.max(logw, axis=1, keepdims=True), inter)
    s = lax.dot_general(q, k, (((1,), (1,)), ((), ())), preferred_element_type=_F32) * jnp.exp(logw - m_t)
    w_inter = jnp.exp(inter - m_t)
    den = (jnp.sum(s, axis=1, keepdims=True)
           + w_inter * jnp.sum(q.astype(_F32) * n_sc[...], axis=1, keepdims=True))
    inv = 1.0 / jnp.maximum(jnp.abs(den), jnp.exp(-m_t))
    sb = s.astype(_BF16)
    kw = k.astype(_F32) * wg_col
    kwb = kw.astype(_BF16)
    n_sc[...] = decay * n_sc[...] + jnp.sum(kw, axis=0, keepdims=True)
    m_sc[...] = m_new
    dv = v.shape[1]
    step = min(dv, 2 * LANES)
    pieces = []
    for c0 in range(0, dv, step):
        cols = slice(c0, c0 + step)
        vc = v[:, cols]
        num = (jnp.dot(sb, vc, preferred_element_type=_F32)
               + w_inter * jnp.dot(q, cb_sc[:, cols], preferred_element_type=_F32))
        pieces.append(num * inv)
        c_new = decay * c_sc[:, cols] + lax.dot_general(kwb, vc, (((0,), (0,)), ((), ())),
                                                        preferred_element_type=_F32)
        c_sc[:, cols] = c_new
        cb_sc[:, cols] = c_new.astype(_BF16)
    return jnp.concatenate(pieces, axis=1)


def _mlstm_scan(q, k, v, gates, uz, xc, hnorm_w, skip, seq, nh):
    bsz, t_all, e = q.shape
    dh = e // nh
    n_ctx = t_all - seq
    ln = _tile(math.gcd(seq, n_ctx), SCAN_CHUNK, 8)
    nc = t_all // ln
    ncl = seq // ln
    ncc = n_ctx // ln
    g = gates.reshape(bsz, nc, ln, 2, 2, nh).transpose(3, 4, 0, 5, 1, 2).reshape(2, 2, bsz, nh, nc, 1, ln)
    act = jax.ShapeDtypeStruct((bsz, t_all, e), _BF16)

    def run(rev, extra_in, extra_specs):
        hb = SCAN_HEADS_PER_STEP[int(rev)]
        hb = hb if nh % hb == 0 else 1
        scratch = [pltpu.VMEM((hb, dh, dh), _F32), pltpu.VMEM((hb, dh, dh), _BF16),
                   pltpu.VMEM((hb, 1, dh), _F32), pltpu.VMEM((hb, 1, 1), _F32)]

        def chunk(s):
            if rev:
                return jnp.where(s < ncc, ncl + ncc - 1 - s, ncl - 1 - (s - ncc))
            return jnp.where(s < ncc, ncl + s, s - ncc)

        tok = lambda col0: pl.BlockSpec((None, ln, hb * dh), lambda b, hh, s: (b, chunk(s), col0 + hh))
        gate = pl.BlockSpec((None, hb, None, 1, ln), lambda b, hh, s: (b, hh, chunk(s), 0, 0))
        return pl.pallas_call(
            functools.partial(_scan_kernel, rev=rev),
            grid=(bsz, nh // hb, nc),
            in_specs=[tok(0), tok(0), tok(0), gate, gate, *extra_specs(tok, hb)],
            out_specs=tok(0),
            out_shape=act,
            scratch_shapes=scratch,
            compiler_params=_cparams("parallel", "parallel", "arbitrary"),
            name="mlstm_scan_rev" if rev else "mlstm_scan_fwd",
        )(q, k, v, g[0, int(rev)], g[1, int(rev)], *extra_in)

    h_fwd = run(False, (), lambda tok, hb: ())
    vec = lambda hb: pl.BlockSpec((1, hb * dh), lambda b, hh, s: (0, hh))
    return run(True, (h_fwd, xc, uz, hnorm_w.reshape(1, e), skip.reshape(1, e)),
               lambda tok, hb: (tok(0), tok(0), tok(nh // hb), vec(hb), vec(hb)))


def _mlstm_branch(uz, conv_w, conv_b, wq, wk, wv, w_ig, b_ig, w_fg, b_fg, hnorm_w, skip, seq):
    nh = w_ig.shape[-1]
    xc, q, k, v, gates = _mlstm_pre(uz, conv_w, conv_b, wq, wk, wv, w_ig, b_ig, w_fg, b_fg, seq)
    return _mlstm_scan(q, k, v, gates, uz, xc, hnorm_w, skip, seq, nh)


def _pool_kernel(u_ref, o_ref, *, seq, n_ctx, tiles_per_group):
    grp = pl.program_id(1) // tiles_per_group
    for gi, w in enumerate(POOL_WINDOWS):
        lo = w // 2
        hi = w - 1 - lo
        assert hi == lo - 1 and lo <= EDGE_ROWS

        def delta(u, bounds, w=w, lo=lo, hi=hi):
            trail = u
            lead = u
            span = 1
            while span < lo:
                trail = trail + _shift_rows(trail, -span, bounds)
                if bounds is not None:
                    lead = lead + _shift_rows(lead, span, bounds)
                span *= 2
            if bounds is None:
                lead = _shift_rows(trail, lo - 1, None) if lo > 1 else u
                return (_shift_rows(trail, -1, None) + lead) * (1.0 / w) - u
            win = _shift_rows(trail, -1, bounds) + lead
            pos, seg_len = bounds
            cnt = jnp.minimum(pos + hi + 1, seg_len) - jnp.maximum(pos - lo, 0)
            return win / cnt.astype(_F32) - u

        @pl.when(grp == gi)
        def _(delta=delta):
            o_ref[...] = delta(u_ref[...].astype(_F32), None).astype(o_ref.dtype)
            _redo_edges(delta, u_ref, o_ref, seq, seq + n_ctx)


def _pool_group_linear(a, w_grp, uz, scale):
    bsz, t_all, e = a.shape
    ng, gw, _ = w_grp.shape
    tm = _tile(t_all, 1088, 16)
    tn = _tile(gw, 1024, LANES)
    npg = gw // tn

    def epilogue(acc, pids, z_ref, sc_ref):
        return acc * sc_ref[...] * _silu(z_ref[...].astype(_F32))

    extras = [uz, scale.reshape(1, e)]
    especs = [pl.BlockSpec((None, tm, tn), lambda b, g, j, i, k: (b, i, (e // tn) + g * npg + j)),
              pl.BlockSpec((1, tn), lambda b, g, j, i, k: (0, g * npg + j))]
    return _matmul(
        (bsz, ng, npg, t_all // tm, 1),
        a, pl.BlockSpec((None, tm, gw), lambda b, g, j, i, k: (b, i, g)),
        w_grp, pl.BlockSpec((None, gw, tn), lambda b, g, j, i, k: (g, 0, j)),
        jax.ShapeDtypeStruct((bsz, t_all, e), _BF16),
        pl.BlockSpec((None, tm, tn), lambda b, g, j, i, k: (b, i, g * npg + j)),
        extras=tuple(extras), extra_specs=tuple(especs), epilogue=epilogue, name="pool_group")


def _pool_branch(uz, w_grp, scale, seq):
    bsz, t_all, e2 = uz.shape
    e = e2 // 2
    te = _tile(e // N_GROUPS, 256, LANES)
    tok = pl.BlockSpec((None, t_all, te), lambda b, j: (b, 0, j))
    dlt = pl.pallas_call(
        functools.partial(_pool_kernel, seq=seq, n_ctx=t_all - seq, tiles_per_group=e // N_GROUPS // te),
        grid=(bsz, e // te),
        in_specs=[tok],
        out_specs=tok,
        out_shape=jax.ShapeDtypeStruct((bsz, t_all, e), _BF16),
        compiler_params=_cparams("parallel", "parallel"),
        name="pool",
    )(uz)
    return _pool_group_linear(dlt, w_grp, uz, scale)


def _cos_sin(n_out, n_in, period):
    ang = (np.outer(np.arange(n_out), np.arange(n_in)) % period) * (2.0 * np.pi / period)
    return np.cos(ang), np.sin(ang)


@functools.lru_cache(maxsize=None)
def _chan_table(gw):
    c, s = _cos_sin(gw, gw, gw)
    return np.concatenate([c, s], axis=0) * gw ** -0.5


@functools.lru_cache(maxsize=None)
def _time_tables(seq, n_ctx):
    n1 = max(f for f in range(1, math.isqrt(seq) + 1) if seq % f == 0)
    n2 = seq // n1
    g = SUBLANES
    assert n1 % g == 0 and n2 % g == 0
    eye = np.eye(g)
    c1, s1 = _cos_sin(n1, n1, n1)
    base = np.block([[c1, -s1], [-s1, -c1]]) * seq ** -0.5
    stage1 = np.einsum('ab,rkst->arkstb', eye, base.reshape(2, n1, 2, n1)).reshape(g * 2 * n1, 2 * n1 * g)
    twc, tws = _cos_sin(n2, n1, seq)
    c2, s2 = _cos_sin(n2, n2, n2)
    cs2 = np.concatenate([c2, s2], axis=1)
    stage2 = np.einsum('ab,krt->kartb', eye, cs2.reshape(n2, 2, n2)).reshape(n2 * g, 2 * n2 * g)
    cc, sc = _cos_sin(n_ctx, n_ctx, n_ctx)
    ctx_tab = np.concatenate([cc, -sc], axis=1) * n_ctx ** -0.5
    lanes = lambda a: np.repeat(a[:, :, None], LANES, axis=2)
    return n1, n2, stage1, lanes(twc), lanes(tws), stage2, ctx_tab


def _time_dft_kernel(y1_ref, y2_ref, z_ref, w1_ref, twc_ref, tws_ref, w2_ref, wc_ref, o_ref,
                     y1f, y2f, ar, ai, out, *, seq, n1, n2):
    g = SUBLANES
    tc = o_ref.shape[-1]
    wide = lambda tab: jnp.concatenate([tab] * (tc // LANES), axis=1)
    y1f[...] = y1_ref[:seq, :].astype(_F32).reshape(y1f.shape)
    y2f[...] = y2_ref[:seq, :].astype(_F32).reshape(y2f.shape)
    for th in range(n2 // g):
        rhs = jnp.concatenate([y1f[:, th].reshape(n1 * g, tc), y2f[:, th].reshape(n1 * g, tc)], axis=0)
        res = jnp.dot(w1_ref[...], rhs.astype(_BF16), preferred_element_type=_F32)
        for tl in range(g):
            t2 = th * g + tl
            a_r = res[tl * 2 * n1:tl * 2 * n1 + n1]
            a_i = res[tl * 2 * n1 + n1:(tl + 1) * 2 * n1]
            c, s = wide(twc_ref[t2]), wide(tws_ref[t2])
            ar[t2] = (a_r * c + a_i * s).reshape(ar.shape[1:])
            ai[t2] = (a_i * c - a_r * s).reshape(ai.shape[1:])
    for kh in range(n1 // g):
        rhs = jnp.concatenate([ar[:, kh].reshape(n2 * g, tc), ai[:, kh].reshape(n2 * g, tc)], axis=0)
        res = jnp.dot(w2_ref[...], rhs.astype(_BF16), preferred_element_type=_F32)
        out[:, kh] = res.reshape(n2, g, tc)
    gate = _silu(z_ref[:seq, :].astype(_F32))
    o_ref[:seq, :] = (out[...].reshape(seq, tc) * gate).astype(o_ref.dtype)
    rhs = jnp.concatenate([y1_ref[seq:, :], y2_ref[seq:, :]], axis=0)
    res = jnp.dot(wc_ref[...], rhs, preferred_element_type=_F32)
    o_ref[seq:, :] = (res * _silu(z_ref[seq:, :].astype(_F32))).astype(o_ref.dtype)


def _fourier_branch(uz, w_grp, seq):
    bsz, t_all, e2 = uz.shape
    n_ctx = t_all - seq
    e = e2 // 2
    gw = e // N_GROUPS
    tab = jnp.asarray(_chan_table(gw), dtype=_BF16)
    tw = _tile(gw, 1024, LANES)
    nrow = gw // tw
    wcomb = _matmul(
        (N_GROUPS, gw // tw, 2 * nrow, 1),
        tab, pl.BlockSpec((tw, gw), lambda g, j, i, k: (i, 0)),
        w_grp, pl.BlockSpec((None, gw, tw), lambda g, j, i, k: (g, 0, j)),
        jax.ShapeDtypeStruct((N_GROUPS, gw, 2 * gw), _BF16),
        pl.BlockSpec((None, tw, tw), lambda g, j, i, k: (g, i % nrow, (i // nrow) * (gw // tw) + j)),
        name="fourier_weight")
    tm = _tile(t_all, 1088, 16)
    tn = _tile(gw, 1024, LANES)
    y = _matmul(
        (bsz, N_GROUPS, 2 * gw // tn, t_all // tm, 1),
        uz, pl.BlockSpec((None, tm, gw), lambda b, g, j, i, k: (b, i, g)),
        wcomb, pl.BlockSpec((None, gw, tn), lambda b, g, j, i, k: (g, 0, j)),
        jax.ShapeDtypeStruct((bsz, t_all, 2 * e), _BF16),
        pl.BlockSpec((None, tm, tn), lambda b, g, j, i, k: (b, i, g * (2 * gw // tn) + j)),
        name="dft_chan")
    n1, n2, stage1, twc, tws, stage2, ctx_tab = _time_tables(seq, n_ctx)
    tc = _tile(gw, 256, LANES)
    cpg = gw // tc

    def ycol(part):
        return lambda b, c: (b, 0, (2 * (c // cpg) + part) * cpg + c % cpg)

    tok = lambda imap: pl.BlockSpec((None, t_all, tc), imap)
    full = lambda a: pl.BlockSpec(a.shape, lambda b, c: (0,) * a.ndim, pipeline_mode=pl.Buffered(1))
    tabs = [jnp.asarray(stage1, dtype=_BF16), jnp.asarray(twc, dtype=_F32), jnp.asarray(tws, dtype=_F32),
            jnp.asarray(stage2, dtype=_BF16), jnp.asarray(ctx_tab, dtype=_BF16)]
    by_t1 = pltpu.VMEM((n1, n2 // SUBLANES, SUBLANES, tc), _F32)
    by_t2 = pltpu.VMEM((n2, n1 // SUBLANES, SUBLANES, tc), _F32)
    return pl.pallas_call(
        functools.partial(_time_dft_kernel, seq=seq, n1=n1, n2=n2),
        grid=(bsz, e // tc),
        in_specs=[tok(ycol(0)), tok(ycol(1)), tok(lambda b, c: (b, 0, e // tc + c))] + [full(t) for t in tabs],
        out_specs=tok(lambda b, c: (b, 0, c)),
        out_shape=jax.ShapeDtypeStruct((bsz, t_all, e), _BF16),
        scratch_shapes=[by_t1, by_t1, by_t2, by_t2, by_t2],
        compiler_params=_cparams("parallel", "parallel"),
        name="dft_time",
    )(y, y, uz, *tabs)


def kernel(x, c, ctx, c_ctx, ada_w, ada_b, norm_g, final_g, a_w_in, a_conv_w, a_conv_b, a_wq, a_wk, a_wv,
           a_w_ig, a_b_ig, a_w_fg, a_b_fg, a_hnorm_w, a_skip, a_w_out, b_w_in, b_w_grp, b_scale, b_w_out,
           c_w_in, c_w_grp, c_w_out):
    bsz, seq, d = x.shape
    depth = ada_w.shape[0]
    n_rows = 8
    assert bsz + 1 <= n_rows
    rows = jnp.concatenate([c, c_ctx[None], jnp.zeros((n_rows - bsz - 1, d), c.dtype)], axis=0)
    ada_b3 = ada_b.reshape(depth, 1, 3 * d)
    mods = _ada(rows, ada_w, ada_b, 1).reshape(n_rows, 1, 3 * d)
    xs = _embed(x, ctx)
    for i in range(depth):
        kind = i % N_MIXERS
        j = i // N_MIXERS
        h = _prenorm(xs, norm_g[i], mods, seq)
        w_in, w_out = ((a_w_in, a_w_out), (b_w_in, b_w_out), (c_w_in, c_w_out))[kind]
        if i + 1 < depth:
            uz, mods_next = _in_proj(h, w_in, j, side=(rows, ada_w, ada_b3, i + 1))
        else:
            uz, mods_next = _in_proj(h, w_in, j), None
        if kind == 0:
            a = _mlstm_branch(uz, a_conv_w[j], a_conv_b[j], a_wq[j], a_wk[j], a_wv[j],
                              a_w_ig[j], a_b_ig[j], a_w_fg[j], a_b_fg[j], a_hnorm_w[j], a_skip[j], seq)
        elif kind == 1:
            a = _pool_branch(uz, b_w_grp[j], b_scale[j], seq)
        else:
            a = _fourier_branch(uz, c_w_grp[j], seq)
        xs = _out_proj(a, w_out, j, xs, mods, seq)
        if mods_next is not None:
            mods = mods_next.reshape(n_rows, 1, 3 * d)
    return _final_norm(xs, final_g, seq)
```
